```python
import math
import jax
import jax.numpy as jnp
from jax import lax
import numpy as np

D_MODEL = 1024
BATCH = 4
SEQ = 8192
DEPTH = 4

PLE_DIM = 256
N_MIXERS = 3
N_A_LAYERS = (DEPTH + 2) // 3
N_B_LAYERS = (DEPTH + 1) // 3
N_C_LAYERS = DEPTH // 3
RMS_EPS = 1e-6
NEG = -1e30

RET_HEADS = 4
RET_DK = D_MODEL // RET_HEADS
RET_DV = 2 * RET_DK
RET_CHUNK = 128
RET_IN = 2 * RET_HEADS * RET_DK + 2 * RET_HEADS * RET_DV

DIL_GROUPS = ((128, 1), (512, 4), (2048, 16))
DIL_HEADS = 8
DIL_HEAD_DIM = D_MODEL // DIL_HEADS
DIL_BLOCK = 128
DIL_QK = len(DIL_GROUPS) * DIL_HEADS * DIL_HEAD_DIM
DIL_IN = 2 * DIL_QK + DIL_HEADS * DIL_HEAD_DIM

DIFF_HEADS = 8
DIFF_HEAD_DIM = D_MODEL // DIFF_HEADS // 2
DIFF_QK = 2 * DIFF_HEADS * DIFF_HEAD_DIM
DIFF_V = DIFF_HEADS * 2 * DIFF_HEAD_DIM
DIFF_IN = 2 * DIFF_QK + DIFF_V
ATTN_BLOCK = 128

FFN_HIDDEN = -(-8 * D_MODEL // (3 * 256)) * 256

kernel_name = "hybrid_retention_dilated_diffattn_trunk"


def rms_norm(x, gain=None, eps=RMS_EPS):
    xf = x.astype(jnp.float32)
    y = xf * lax.rsqrt(jnp.mean(xf * xf, axis=-1, keepdims=True) + eps)
    if gain is not None:
        y = y * gain.astype(jnp.float32)
    return y.astype(x.dtype)


def alibi_slopes(n):
    ratio = 2.0 ** (-8.0 / n)
    return jnp.asarray(np.array([ratio ** (h + 1) for h in range(n)], dtype=np.float32))


def diff_lambda_init(layer_idx):
    return 0.8 - 0.6 * math.exp(-0.3 * layer_idx)


def retention(hn, w_in, w_out):
    B_, S, _ = hn.shape
    H, dk, dv, C = RET_HEADS, RET_DK, RET_DV, RET_CHUNK
    proj = hn @ w_in
    o1, o2, o3 = H * dk, 2 * H * dk, 2 * H * dk + H * dv
    q = proj[..., :o1].reshape(B_, S, H, dk).astype(jnp.float32)
    k = proj[..., o1:o2].reshape(B_, S, H, dk).astype(jnp.float32) * (dk ** -0.5)
    v = proj[..., o2:o3].reshape(B_, S, H, dv).astype(jnp.float32)
    g = proj[..., o3:]
    n_chunks = S // C

    def chunks(a):
        return a.reshape(B_, n_chunks, C, H, a.shape[-1]).transpose(1, 0, 3, 2, 4)

    log_g = jnp.log(1.0 - 2.0 ** (-5.0 - jnp.arange(H, dtype=jnp.float32)))
    pos = jnp.arange(C, dtype=jnp.float32)
    rel = pos[:, None] - pos[None, :]
    decay_in = jnp.where(rel >= 0, jnp.exp(jnp.maximum(rel, 0.0)[None] * log_g[:, None, None]), 0.0)
    decay_q = jnp.exp((pos + 1.0)[None] * log_g[:, None])
    decay_k = jnp.exp((C - 1.0 - pos)[None] * log_g[:, None])
    decay_chunk = jnp.exp(C * log_g)

    def step(R, inp):
        qc, kc, vc = inp
        att = jnp.einsum('bhnd,bhmd->bhnm', qc, kc) * decay_in
        y = (jnp.einsum('bhnm,bhme->bhne', att, vc)
             + jnp.einsum('bhnd,bhde->bhne', qc, R) * decay_q[None, :, :, None])
        R = (R * decay_chunk[None, :, None, None]
             + jnp.einsum('bhmd,bhme->bhde', kc * decay_k[None, :, :, None], vc))
        return R, y

    R0 = jnp.zeros((B_, H, dk, dv), jnp.float32)
    _, ys = lax.scan(step, R0, (chunks(q), chunks(k), chunks(v)))
    y = ys.transpose(1, 0, 3, 2, 4).reshape(B_, S, H, dv)
    y = rms_norm(y).reshape(B_, S, H * dv)
    y = (jax.nn.silu(g) * y).astype(hn.dtype)
    return y @ w_out


def dilated_group(q, k, v, window, dilation, slopes):
    B_, S, H, hd = q.shape
    J = window // dilation
    unit = dilation * DIL_BLOCK
    Sp = -(-S // unit) * unit
    L = Sp // dilation
    nb = L // DIL_BLOCK

    def to_streams(a):
        a = jnp.pad(a, ((0, 0), (0, Sp - S), (0, 0), (0, 0)))
        a = a.reshape(B_, L, dilation, H, hd).transpose(0, 2, 3, 1, 4)
        return a.reshape(B_, dilation, H, nb, DIL_BLOCK, hd)

    def with_prev(a):
        prev = jnp.pad(a[:, :, :, :-1], ((0, 0), (0, 0), (0, 0), (1, 0), (0, 0), (0, 0)))
        return jnp.concatenate([prev, a], axis=4)

    qs = to_streams(q)
    kc = with_prev(to_streams(k))
    vc = with_prev(to_streams(v))
    s = jnp.einsum('brhnqd,brhnkd->brhnqk', qs, kc).astype(jnp.float32) * (hd ** -0.5)
    rel = (DIL_BLOCK + jnp.arange(DIL_BLOCK))[:, None] - jnp.arange(2 * DIL_BLOCK)[None, :]
    first = (jnp.arange(nb)[:, None, None] == 0) & (jnp.arange(2 * DIL_BLOCK)[None, None, :] < DIL_BLOCK)
    valid = ((rel >= 0) & (rel <= J))[None] & jnp.logical_not(first)
    bias = -(slopes[:, None, None] * (rel * dilation).astype(jnp.float32)[None])
    s = jnp.where(valid, s + bias[:, None], NEG)
    lse = jax.nn.logsumexp(s, axis=-1)
    pr = jnp.exp(s - lse[..., None]).astype(v.dtype)
    o = jnp.einsum('brhnqk,brhnkd->brhnqd', pr, vc)
    o = o.reshape(B_, dilation, H, L, hd).transpose(0, 3, 1, 2, 4).reshape(B_, Sp, H, hd)[:, :S]
    lse = lse.reshape(B_, dilation, H, L).transpose(0, 3, 1, 2).reshape(B_, Sp, H)[:, :S]
    return o, lse


def dilated_attention(hn, w_in, w_out, q_gain, k_gain):
    B_, S, _ = hn.shape
    NG = len(DIL_GROUPS)
    proj = hn @ w_in
    q = proj[..., :DIL_QK].reshape(B_, S, NG, DIL_HEADS, DIL_HEAD_DIM)
    k = proj[..., DIL_QK:2 * DIL_QK].reshape(B_, S, NG, DIL_HEADS, DIL_HEAD_DIM)
    v = proj[..., 2 * DIL_QK:].reshape(B_, S, DIL_HEADS, DIL_HEAD_DIM)
    q = rms_norm(q, q_gain)
    k = rms_norm(k, k_gain)
    slopes = alibi_slopes(DIL_HEADS)
    outs, lses = [], []
    for gi, (window, dilation) in enumerate(DIL_GROUPS):
        o, l = dilated_group(q[:, :, gi], k[:, :, gi], v, window, dilation, slopes)
        outs.append(o)
        lses.append(l)
    w = jax.nn.softmax(jnp.stack(lses, axis=0), axis=0)
    o = jnp.einsum('gbsh,gbshd->bshd', w, jnp.stack(outs, axis=0).astype(jnp.float32))
    return o.astype(hn.dtype).reshape(B_, S, DIL_HEADS * DIL_HEAD_DIM) @ w_out


def diff_attention(hn, w_in, w_out, q_gain, k_gain, lq1, lk1, lq2, lk2, subln_gain, lambda_init):
    B_, S, _ = hn.shape
    proj = hn @ w_in
    q = proj[..., :DIFF_QK].reshape(B_, S, 2 * DIFF_HEADS, DIFF_HEAD_DIM)
    k = proj[..., DIFF_QK:2 * DIFF_QK].reshape(B_, S, 2 * DIFF_HEADS, DIFF_HEAD_DIM)
    v = proj[..., 2 * DIFF_QK:].reshape(B_, S, DIFF_HEADS, 2 * DIFF_HEAD_DIM)
    q = rms_norm(q, q_gain)
    k = rms_norm(k, k_gain)
    f32 = jnp.float32
    lam = (jnp.exp(jnp.sum(lq1.astype(f32) * lk1.astype(f32)))
           - jnp.exp(jnp.sum(lq2.astype(f32) * lk2.astype(f32))) + lambda_init)
    slopes = jnp.repeat(alibi_slopes(DIFF_HEADS), 2)
    kpos = jnp.arange(S, dtype=f32)
    scale = DIFF_HEAD_DIM ** -0.5

    def block(n):
        qb = lax.dynamic_slice_in_dim(q, n * ATTN_BLOCK, ATTN_BLOCK, axis=1)
        s = jnp.einsum('bqhd,bkhd->bhqk', qb, k).astype(f32) * scale
        qpos = (n * ATTN_BLOCK + jnp.arange(ATTN_BLOCK)).astype(f32)
        rel = qpos[:, None] - kpos[None, :]
        s = jnp.where(rel >= 0, s - slopes[:, None, None] * rel, NEG)
        a = jax.nn.softmax(s, axis=-1).reshape(B_, DIFF_HEADS, 2, ATTN_BLOCK, S)
        w = (a[:, :, 0] - lam * a[:, :, 1]).astype(v.dtype)
        return jnp.einsum('bhqk,bkhe->bqhe', w, v)

    o = lax.map(block, jnp.arange(S // ATTN_BLOCK))
    o = o.transpose(1, 0, 2, 3, 4).reshape(B_, S, DIFF_HEADS, 2 * DIFF_HEAD_DIM)
    o = rms_norm(o, subln_gain) * (1.0 - lambda_init)
    return o.reshape(B_, S, DIFF_HEADS * 2 * DIFF_HEAD_DIM) @ w_out


def swiglu(hn, w_in, w_out):
    u = hn @ w_in
    a, b = u[..., :FFN_HIDDEN], u[..., FFN_HIDDEN:]
    return (jax.nn.silu(a) * b) @ w_out


def setup_inputs(seed: int = 0) -> dict:
    key = jax.random.key(seed)
    ks = jax.random.split(key, 32)
    f32 = jnp.float32

    def dense(k, shape):
        return jax.random.normal(k, shape, f32) * (shape[-2] ** -0.5)

    def gain(k, shape):
        return 1.0 + 0.02 * jax.random.normal(k, shape, f32)

    def small(k, shape, scale):
        return scale * jax.random.normal(k, shape, f32)

    return {
        "x": jax.random.normal(ks[0], (BATCH, SEQ, D_MODEL), f32),
        "p": jax.random.normal(ks[1], (DEPTH, BATCH, SEQ, PLE_DIM), f32),
        "mix_norm": gain(ks[2], (DEPTH, D_MODEL)),
        "ffn_norm": gain(ks[3], (DEPTH, D_MODEL)),
        "a_w_in": dense(ks[4], (N_A_LAYERS, D_MODEL, RET_IN)),
        "a_w_out": dense(ks[5], (N_A_LAYERS, RET_HEADS * RET_DV, D_MODEL)),
        "b_w_in": dense(ks[6], (N_B_LAYERS, D_MODEL, DIL_IN)),
        "b_q_norm": gain(ks[7], (N_B_LAYERS, DIL_HEAD_DIM)),
        "b_k_norm": gain(ks[8], (N_B_LAYERS, DIL_HEAD_DIM)),
        "b_w_out": dense(ks[9], (N_B_LAYERS, DIL_HEADS * DIL_HEAD_DIM, D_MODEL)),
        "c_w_in": dense(ks[10], (N_C_LAYERS, D_MODEL, DIFF_IN)),
        "c_q_norm": gain(ks[11], (N_C_LAYERS, DIFF_HEAD_DIM)),
        "c_k_norm": gain(ks[12], (N_C_LAYERS, DIFF_HEAD_DIM)),
        "c_lambda_q1": small(ks[13], (N_C_LAYERS, DIFF_HEAD_DIM), 0.1),
        "c_lambda_k1": small(ks[14], (N_C_LAYERS, DIFF_HEAD_DIM), 0.1),
        "c_lambda_q2": small(ks[15], (N_C_LAYERS, DIFF_HEAD_DIM), 0.1),
        "c_lambda_k2": small(ks[16], (N_C_LAYERS, DIFF_HEAD_DIM), 0.1),
        "c_subln": gain(ks[17], (N_C_LAYERS, 2 * DIFF_HEAD_DIM)),
        "c_w_out": dense(ks[18], (N_C_LAYERS, DIFF_V, D_MODEL)),
        "ffn_w_in": dense(ks[19], (DEPTH, D_MODEL, 2 * FFN_HIDDEN)),
        "ffn_w_out": dense(ks[20], (DEPTH, FFN_HIDDEN, D_MODEL)),
        "ple_w_proj": dense(ks[21], (DEPTH, PLE_DIM, D_MODEL)),
        "ple_norm": gain(ks[22], (DEPTH, D_MODEL)),
        "ple_gate_norm": gain(ks[23], (DEPTH, D_MODEL)),
        "ple_w_gate": dense(ks[24], (DEPTH, D_MODEL, D_MODEL)),
    }


def reference(x, p, mix_norm, ffn_norm, a_w_in, a_w_out, b_w_in, b_q_norm, b_k_norm, b_w_out,
              c_w_in, c_q_norm, c_k_norm, c_lambda_q1, c_lambda_k1, c_lambda_q2, c_lambda_k2,
              c_subln, c_w_out, ffn_w_in, ffn_w_out, ple_w_proj, ple_norm, ple_gate_norm, ple_w_gate):
    h = x
    for i in range(DEPTH):
        kind, j = i % N_MIXERS, i // N_MIXERS
        hn = rms_norm(h, mix_norm[i])
        if kind == 0:
            y = retention(hn, a_w_in[j], a_w_out[j])
        elif kind == 1:
            y = dilated_attention(hn, b_w_in[j], b_w_out[j], b_q_norm[j], b_k_norm[j])
        else:
            y = diff_attention(hn, c_w_in[j], c_w_out[j], c_q_norm[j], c_k_norm[j],
                               c_lambda_q1[j], c_lambda_k1[j], c_lambda_q2[j], c_lambda_k2[j],
                               c_subln[j], diff_lambda_init(i))
        h = h + y.astype(h.dtype)
        h = h + swiglu(rms_norm(h, ffn_norm[i]), ffn_w_in[i], ffn_w_out[i]).astype(h.dtype)
        gate = jax.nn.sigmoid(rms_norm(h, ple_gate_norm[i]) @ ple_w_gate[i])
        e = rms_norm(p[i] @ ple_w_proj[i], ple_norm[i])
        h = h + (gate * e).astype(h.dtype)
    return h
```

```python
import functools
import math

import numpy as np
import jax
import jax.numpy as jnp
from jax import lax
from jax.experimental import pallas as pl
from jax.experimental.pallas import tpu as pltpu

D_MODEL = 1024
PLE_DIM = 256
N_MIXERS = 3
RMS_EPS = 1e-6
NEG = -1e30

RET_HEADS = 4
RET_DK = 256
RET_DV = 512
RET_CHUNK = 128
RET_IN = 2 * RET_HEADS * RET_DK + 2 * RET_HEADS * RET_DV

DIL_GROUPS = ((128, 1), (512, 4), (2048, 16))
DIL_HEADS = 8
DIL_HEAD_DIM = 128
DIL_BLOCK = 128
DIL_QK = len(DIL_GROUPS) * DIL_HEADS * DIL_HEAD_DIM
DIL_IN = 2 * DIL_QK + DIL_HEADS * DIL_HEAD_DIM

DIFF_HEADS = 8
DIFF_HEAD_DIM = 64
DIFF_QK = 2 * DIFF_HEADS * DIFF_HEAD_DIM
DIFF_V = DIFF_HEADS * 2 * DIFF_HEAD_DIM
DIFF_IN = 2 * DIFF_QK + DIFF_V

FFN_HIDDEN = 2816

LANES = 128
V7X_VMEM_BYTES = 64 * 1024 * 1024
VMEM_LIMIT_BYTES = V7X_VMEM_BYTES - 8 * 1024 * 1024

PROJ_TM = 512
PROJ_TN = 1024
FFN_TM = 512
FFN_TK = 1408
RET_T = 512
DIL_WINDOW = 2048
DIFF_TQ = 1024
DIFF_TK = 512
OUT_TM = 512

F32 = jnp.float32
BF16 = jnp.bfloat16

_NT = (((1,), (1,)), ((), ()))
_TN = (((0,), (0,)), ((), ()))


def _params(semantics):
    return pltpu.CompilerParams(dimension_semantics=semantics, vmem_limit_bytes=VMEM_LIMIT_BYTES)


def _rms(x):
    return x * lax.rsqrt(jnp.mean(x * x, axis=-1, keepdims=True) + RMS_EPS)


def _sigmoid(x):
    return 1.0 / (1.0 + jnp.exp(-x))


def _norm_proj_kernel(x_ref, g_ref, w_ref, hg_ref, o_ref, xn_ref, *, head_norm, n_norm_tiles):
    j = pl.program_id(1)

    @pl.when(j == 0)
    def _():
        xn_ref[...] = (_rms(x_ref[...]) * g_ref[...]).astype(BF16)

    y = jnp.dot(xn_ref[...], w_ref[...], preferred_element_type=F32)
    slabs = y.shape[1] // LANES

    def store(fn):
        for s in range(slabs):
            o_ref[s] = fn(y[:, s * LANES:(s + 1) * LANES]).astype(o_ref.dtype)

    def full_norm(z):
        return _rms(z) * hg_ref[...]

    def half_norm(z):
        lo = lax.broadcasted_iota(jnp.int32, z.shape, 1) < (LANES // 2)
        zz = z * z
        ss_lo = jnp.sum(jnp.where(lo, zz, 0.0), axis=-1, keepdims=True)
        ss_hi = jnp.sum(jnp.where(lo, 0.0, zz), axis=-1, keepdims=True)
        inv = jnp.where(lo, lax.rsqrt(ss_lo / (LANES // 2) + RMS_EPS),
                        lax.rsqrt(ss_hi / (LANES // 2) + RMS_EPS))
        return z * inv * hg_ref[...]

    if head_norm is None:
        store(lambda z: z)
    else:
        fn = full_norm if head_norm == "full" else half_norm
        pl.when(j < n_norm_tiles)(lambda: store(fn))
        pl.when(j >= n_norm_tiles)(lambda: store(lambda z: z))


def _norm_proj(h, gain, w, head_gains=None, head_norm=None, n_norm_tiles=0):
    n, d = h.shape
    n_out = w.shape[1]
    n_tiles = n_out // PROJ_TN
    slabs = PROJ_TN // LANES
    if head_gains is None:
        head_gains = jnp.ones((n_tiles, 1, LANES), F32)
    kern = functools.partial(_norm_proj_kernel, head_norm=head_norm, n_norm_tiles=n_norm_tiles)
    return pl.pallas_call(
        kern,
        grid=(n // PROJ_TM, n_tiles),
        in_specs=[
            pl.BlockSpec((PROJ_TM, d), lambda i, j: (i, 0)),
            pl.BlockSpec((1, d), lambda i, j: (0, 0)),
            pl.BlockSpec((d, PROJ_TN), lambda i, j: (0, j)),
            pl.BlockSpec((None, 1, LANES), lambda i, j: (j, 0, 0)),
        ],
        out_specs=pl.BlockSpec((slabs, PROJ_TM, LANES), lambda i, j: (j, i, 0)),
        out_shape=jax.ShapeDtypeStruct((n_out // LANES, n, LANES), BF16),
        scratch_shapes=[pltpu.VMEM((PROJ_TM, d), BF16)],
        compiler_params=_params(("parallel", "arbitrary")),
        name="norm_proj",
    )(h, gain.reshape(1, d), w, head_gains)


def _retention_tables():
    h = np.arange(RET_HEADS, dtype=np.float32)
    log_g = np.log(np.float32(1.0) - np.float32(2.0) ** (np.float32(-5.0) - h)).astype(np.float32)
    pos = np.arange(RET_CHUNK, dtype=np.float32)
    rel = pos[:, None] - pos[None, :]
    scale = np.float32(RET_DK ** -0.5)
    din = np.where(rel >= 0, np.exp(np.maximum(rel, 0.0)[None] * log_g[:, None, None]), 0.0)
    dq = np.exp((pos + 1.0)[None] * log_g[:, None])
    dk = np.exp((RET_CHUNK - 1.0 - pos)[None] * log_g[:, None])
    dchunk = np.exp(RET_CHUNK * log_g)
    return ((din * scale).astype(np.float32), dq.astype(np.float32)[:, :, None],
            (dk * scale).astype(np.float32)[:, :, None], tuple(float(v) for v in dchunk))


def _retention_kernel(q_ref, k_ref, v_ref, g_ref, h_ref, wo_ref, din_ref, dq_ref, dk_ref,
                      o_ref, r_ref, y_ref, *, dchunk):
    @pl.when(pl.program_id(1) == 0)
    def _():
        r_ref[...] = jnp.zeros_like(r_ref)

    qs, vs = RET_DK // LANES, RET_DV // LANES

    def cat(ref, first, count, rows):
        return jnp.concatenate([ref[first + s, rows, :] for s in range(count)], axis=-1)

    def chunk(c, carry):
        rows = pl.ds(pl.multiple_of(c * RET_CHUNK, RET_CHUNK), RET_CHUNK)
        for hh in range(RET_HEADS):
            qc = cat(q_ref, hh * qs, qs, rows)
            kc = cat(k_ref, hh * qs, qs, rows)
            vc = cat(v_ref, hh * vs, vs, rows)
            gc = cat(g_ref, hh * vs, vs, rows).astype(F32)
            att = lax.dot_general(qc, kc, _NT, preferred_element_type=F32) * din_ref[hh]
            state = r_ref[hh]
            y = (jnp.dot(att.astype(BF16), vc, preferred_element_type=F32)
                 + jnp.dot(qc, state.astype(BF16), preferred_element_type=F32) * dq_ref[hh])
            kd = (kc.astype(F32) * dk_ref[hh]).astype(BF16)
            r_ref[hh] = state * dchunk[hh] + lax.dot_general(kd, vc, _TN,
                                                             preferred_element_type=F32)
            y_ref[rows, hh * RET_DV:(hh + 1) * RET_DV] = (gc * _sigmoid(gc) * _rms(y)).astype(BF16)
        return carry

    lax.fori_loop(0, y_ref.shape[0] // RET_CHUNK, chunk, 0)
    o_ref[...] = h_ref[...] + jnp.dot(y_ref[...], wo_ref[...], preferred_element_type=F32)


def _retention(proj, h, w_out, batch, seq):
    n, d = h.shape
    din, dq, dk, dchunk = _retention_tables()
    p4 = proj.reshape(proj.shape[0], batch, seq, LANES)
    h3 = h.reshape(batch, seq, d)
    nq = RET_HEADS * RET_DK // LANES
    nv = RET_HEADS * RET_DV // LANES
    t = RET_T
    slab_spec = lambda cnt, blk: pl.BlockSpec((cnt, None, t, LANES), lambda b, i: (blk, b, i, 0))
    const3 = lambda a: pl.BlockSpec(a.shape, lambda b, i: (0, 0, 0))
    out = pl.pallas_call(
        functools.partial(_retention_kernel, dchunk=dchunk),
        grid=(batch, seq // t),
        in_specs=[
            slab_spec(nq, 0), slab_spec(nq, 1), slab_spec(nv, 1), slab_spec(nv, 2),
            pl.BlockSpec((None, t, d), lambda b, i: (b, i, 0)),
            pl.BlockSpec(w_out.shape, lambda b, i: (0, 0)),
            const3(din), const3(dq), const3(dk),
        ],
        out_specs=pl.BlockSpec((None, t, d), lambda b, i: (b, i, 0)),
        out_shape=jax.ShapeDtypeStruct((batch, seq, d), F32),
        scratch_shapes=[pltpu.VMEM((RET_HEADS, RET_DK, RET_DV), F32),
                        pltpu.VMEM((t, RET_HEADS * RET_DV), BF16)],
        compiler_params=_params(("parallel", "arbitrary")),
        name="retention",
    )(p4, p4, p4, p4, h3, w_out, jnp.asarray(din), jnp.asarray(dq), jnp.asarray(dk))
    return out.reshape(n, d)


def _alibi_slopes(n):
    ratio = 2.0 ** (-8.0 / n)
    return np.array([ratio ** (i + 1) for i in range(n)], dtype=np.float32)


def _dilated_tables(dilation):
    slopes = _alibi_slopes(DIL_HEADS)
    qi = np.arange(DIL_BLOCK)[:, None]
    kj = np.arange(DIL_BLOCK)[None, :]
    span = DIL_GROUPS[0][0] // DIL_GROUPS[0][1]
    rel_prev = DIL_BLOCK + qi - kj
    rel_cur = qi - kj

    def tab(rel):
        valid = (rel >= 0) & (rel <= span)
        bias = -(slopes[:, None, None] * (rel * dilation).astype(np.float32)[None])
        return np.where(valid[None], bias, np.float32(NEG)).astype(np.float32)

    return tab(rel_prev), tab(rel_cur)


def _dilated_kernel(q_ref, kh_ref, k_ref, vh_ref, v_ref, tp_ref, tc_ref, o_ref, l_ref,
                    *, dilation, n_blocks):
    scale = DIL_HEAD_DIM ** -0.5
    tab_prev = tp_ref[...]
    tab_cur = tc_ref[...]
    tab_halo = jnp.where(pl.program_id(2) > 0, tab_prev, NEG)
    for r in range(dilation):
        lanes = slice(r * LANES, (r + 1) * LANES)
        for blk in range(n_blocks):
            rows = slice(blk * DIL_BLOCK, (blk + 1) * DIL_BLOCK)
            qb = q_ref[rows, lanes]
            if blk == 0:
                kp, vp, tp = kh_ref[:, lanes], vh_ref[:, lanes], tab_halo
            else:
                prev = slice((blk - 1) * DIL_BLOCK, blk * DIL_BLOCK)
                kp, vp, tp = k_ref[prev, lanes], v_ref[prev, lanes], tab_prev
            kc, vc = k_ref[rows, lanes], v_ref[rows, lanes]
            sp = lax.dot_general(qb, kp, _NT, preferred_element_type=F32) * scale + tp
            sc = lax.dot_general(qb, kc, _NT, preferred_element_type=F32) * scale + tab_cur
            m = jnp.maximum(jnp.max(sp, axis=-1, keepdims=True), jnp.max(sc, axis=-1, keepdims=True))
            pp = jnp.exp(sp - m)
            pc = jnp.exp(sc - m)
            den = jnp.sum(pp, axis=-1, keepdims=True) + jnp.sum(pc, axis=-1, keepdims=True)
            num = (jnp.dot(pp.astype(BF16), vp, preferred_element_type=F32)
                   + jnp.dot(pc.astype(BF16), vc, preferred_element_type=F32))
            o_ref[rows, lanes] = (num / den).astype(o_ref.dtype)
            l_ref[rows, lanes] = jnp.broadcast_to(m + jnp.log(den), (DIL_BLOCK, LANES))


def _dilated_group(proj, gi, batch, seq):
    _, dilation = DIL_GROUPS[gi]
    n = batch * seq
    rows = DIL_WINDOW // dilation
    width = dilation * LANES
    n_blocks = rows // DIL_BLOCK
    view = proj.reshape(proj.shape[0], batch, seq // dilation, width)
    tab_prev, tab_cur = _dilated_tables(dilation)
    q0 = gi * DIL_HEADS
    k0 = len(DIL_GROUPS) * DIL_HEADS + gi * DIL_HEADS
    v0 = 2 * len(DIL_GROUPS) * DIL_HEADS

    def cur(first):
        return pl.BlockSpec((None, None, rows, width), lambda b, hh, w: (first + hh, b, w, 0))

    def halo(first):
        return pl.BlockSpec((None, None, DIL_BLOCK, width),
                            lambda b, hh, w: (first + hh, b, jnp.maximum(w * n_blocks - 1, 0), 0))

    tab_spec = pl.BlockSpec((None, DIL_BLOCK, DIL_BLOCK), lambda b, hh, w: (hh, 0, 0))
    out_spec = pl.BlockSpec((None, None, rows, width), lambda b, hh, w: (hh, b, w, 0))
    shape = (DIL_HEADS, batch, seq // dilation, width)
    o, lse = pl.pallas_call(
        functools.partial(_dilated_kernel, dilation=dilation, n_blocks=n_blocks),
        grid=(batch, DIL_HEADS, seq // DIL_WINDOW),
        in_specs=[cur(q0), halo(k0), cur(k0), halo(v0), cur(v0), tab_spec, tab_spec],
        out_specs=[out_spec, out_spec],
        out_shape=[jax.ShapeDtypeStruct(shape, BF16), jax.ShapeDtypeStruct(shape, F32)],
        compiler_params=_params(("parallel", "parallel", "arbitrary")),
        name=f"dilated_g{gi}",
    )(view, view, view, view, view, jnp.asarray(tab_prev), jnp.asarray(tab_cur))
    return o.reshape(DIL_HEADS, n, LANES), lse.reshape(DIL_HEADS, n, LANES)


def _dilated_out_kernel(o0, o1, o2, l0, l1, l2, h_ref, w_ref, out_ref):
    heads = []
    for hh in range(DIL_HEADS):
        a, b, c = l0[hh], l1[hh], l2[hh]
        m = jnp.maximum(jnp.maximum(a, b), c)
        ea, eb, ec = jnp.exp(a - m), jnp.exp(b - m), jnp.exp(c - m)
        mix = (ea * o0[hh].astype(F32) + eb * o1[hh].astype(F32) + ec * o2[hh].astype(F32))
        heads.append((mix / (ea + eb + ec)).astype(BF16))
    o = jnp.concatenate(heads, axis=-1)
    out_ref[...] = h_ref[...] + jnp.dot(o, w_ref[...], preferred_element_type=F32)


def _dilated_out(os, ls, h, w_out):
    n, d = h.shape
    tm = OUT_TM
    slab = pl.BlockSpec((DIL_HEADS, tm, LANES), lambda i: (0, i, 0))
    return pl.pallas_call(
        _dilated_out_kernel,
        grid=(n // tm,),
        in_specs=[slab] * 6 + [pl.BlockSpec((tm, d), lambda i: (i, 0)),
                               pl.BlockSpec(w_out.shape, lambda i: (0, 0))],
        out_specs=pl.BlockSpec((tm, d), lambda i: (i, 0)),
        out_shape=jax.ShapeDtypeStruct((n, d), F32),
        compiler_params=_params(("parallel",)),
        name="dilated_out",
    )(*os, *ls, h, w_out)


def _diff_steps(seq):
    ratio = DIFF_TQ // DIFF_TK
    qi, ki = [], []
    for a in range(seq // DIFF_TQ):
        for b in range(ratio * (a + 1)):
            qi.append(a)
            ki.append(b)
    return np.asarray(qi, np.int32), np.asarray(ki, np.int32)


def _diff_kernel(qi_tab, ki_tab, q_ref, k_ref, v_ref, slope_ref, lq1, lk1, lq2, lk2, sg_ref,
                 o_ref, tab_ref, m1, l1, a1, m2, l2, a2, *, lambda_init):
    step = pl.program_id(2)
    qi = qi_tab[step]
    ki = ki_tab[step]
    ratio = DIFF_TQ // DIFF_TK
    scale = DIFF_HEAD_DIM ** -0.5
    slope = slope_ref[...]

    @pl.when(step == 0)
    def _():
        key = lax.broadcasted_iota(jnp.int32, (DIFF_TK, DIFF_TQ), 0)
        qry = lax.broadcasted_iota(jnp.int32, (DIFF_TK, DIFF_TQ), 1)
        pen = slope * (qry - key).astype(F32)
        tab_ref[0] = pen
        for j in range(ratio):
            tab_ref[1 + j] = jnp.where(qry - key - j * DIFF_TK >= 0, pen, -NEG)

    @pl.when(ki == 0)
    def _():
        for m, l, a in ((m1, l1, a1), (m2, l2, a2)):
            m[...] = jnp.full_like(m, NEG)
            l[...] = jnp.zeros_like(l)
            a[...] = jnp.zeros_like(a)

    diag = ki - ratio * qi
    tab = tab_ref[jnp.where(diag < 0, 0, diag + 1)]
    offset = slope * (qi * DIFF_TQ - ki * DIFF_TK).astype(F32)

    q = q_ref[...]
    lo = lax.broadcasted_iota(jnp.int32, q.shape, 1) < DIFF_HEAD_DIM
    zero = jnp.zeros_like(q)
    k = k_ref[...]
    v = v_ref[...]
    for qz, m_ref, l_ref, a_ref in ((jnp.where(lo, q, zero), m1, l1, a1),
                                    (jnp.where(lo, zero, q), m2, l2, a2)):
        t = lax.dot_general(k, qz, _NT, preferred_element_type=F32) * scale - tab
        m_old = m_ref[...]
        m_new = jnp.maximum(m_old, jnp.max(t, axis=0, keepdims=True) - offset)
        alpha = jnp.exp(m_old - m_new)
        p = jnp.exp(t - (m_new + offset))
        l_ref[...] = alpha * l_ref[...] + jnp.sum(p, axis=0, keepdims=True)
        a_ref[...] = a_ref[...] * alpha + lax.dot_general(v, p.astype(BF16), _TN,
                                                          preferred_element_type=F32)
        m_ref[...] = m_new

    @pl.when(diag == ratio - 1)
    def _():
        lam = (jnp.exp(jnp.sum(lq1[...] * lk1[...], axis=-1, keepdims=True))
               - jnp.exp(jnp.sum(lq2[...] * lk2[...], axis=-1, keepdims=True)) + lambda_init)
        o_t = a1[...] / l1[...] - lam * (a2[...] / l2[...])
        o = _rms(o_t.T) * sg_ref[...] * (1.0 - lambda_init)
        o_ref[...] = o.astype(o_ref.dtype)


def _diff_attention(proj, lq1, lk1, lq2, lk2, subln, lambda_init, batch, seq):
    n = batch * seq
    p4 = proj.reshape(proj.shape[0], batch, seq, LANES)
    qi_tab, ki_tab = _diff_steps(seq)
    slopes = np.repeat(_alibi_slopes(DIFF_HEADS)[:, None, None], DIFF_TQ, axis=2)
    vec = lambda a: a.reshape(1, -1).astype(F32)
    small = lambda a: pl.BlockSpec(a.shape, lambda b, hh, s, qt, kt: (0, 0))
    lqs = [vec(lq1), vec(lk1), vec(lq2), vec(lk2)]
    sg = vec(subln)
    grid_spec = pltpu.PrefetchScalarGridSpec(
        num_scalar_prefetch=2,
        grid=(batch, DIFF_HEADS, len(qi_tab)),
        in_specs=[
            pl.BlockSpec((None, None, DIFF_TQ, LANES), lambda b, hh, s, qt, kt: (hh, b, qt[s], 0)),
            pl.BlockSpec((None, None, DIFF_TK, LANES),
                         lambda b, hh, s, qt, kt: (DIFF_HEADS + hh, b, kt[s], 0)),
            pl.BlockSpec((None, None, DIFF_TK, LANES),
                         lambda b, hh, s, qt, kt: (2 * DIFF_HEADS + hh, b, kt[s], 0)),
            pl.BlockSpec((None, 1, DIFF_TQ), lambda b, hh, s, qt, kt: (hh, 0, 0)),
            small(lqs[0]), small(lqs[1]), small(lqs[2]), small(lqs[3]), small(sg),
        ],
        out_specs=pl.BlockSpec((None, None, DIFF_TQ, LANES), lambda b, hh, s, qt, kt: (hh, b, qt[s], 0)),
        scratch_shapes=[
            pltpu.VMEM((1 + DIFF_TQ // DIFF_TK, DIFF_TK, DIFF_TQ), F32),
            pltpu.VMEM((1, DIFF_TQ), F32), pltpu.VMEM((1, DIFF_TQ), F32),
            pltpu.VMEM((2 * DIFF_HEAD_DIM, DIFF_TQ), F32),
            pltpu.VMEM((1, DIFF_TQ), F32), pltpu.VMEM((1, DIFF_TQ), F32),
            pltpu.VMEM((2 * DIFF_HEAD_DIM, DIFF_TQ), F32),
        ],
    )
    o = pl.pallas_call(
        functools.partial(_diff_kernel, lambda_init=lambda_init),
        grid_spec=grid_spec,
        out_shape=jax.ShapeDtypeStruct((DIFF_HEADS, batch, seq, LANES), BF16),
        compiler_params=_params(("parallel", "parallel", "arbitrary")),
        name="diff_attention",
    )(jnp.asarray(qi_tab), jnp.asarray(ki_tab), p4, p4, p4, jnp.asarray(slopes), *lqs, sg)
    return o.reshape(DIFF_HEADS, n, LANES)


def _slab_out_kernel(o_ref, h_ref, w_ref, out_ref):
    o = jnp.concatenate([o_ref[s] for s in range(o_ref.shape[0])], axis=-1)
    out_ref[...] = h_ref[...] + jnp.dot(o, w_ref[...], preferred_element_type=F32)


def _slab_out(o, h, w_out):
    n, d = h.shape
    tm = OUT_TM
    return pl.pallas_call(
        _slab_out_kernel,
        grid=(n // tm,),
        in_specs=[pl.BlockSpec((o.shape[0], tm, LANES), lambda i: (0, i, 0)),
                  pl.BlockSpec((tm, d), lambda i: (i, 0)),
                  pl.BlockSpec(w_out.shape, lambda i: (0, 0))],
        out_specs=pl.BlockSpec((tm, d), lambda i: (i, 0)),
        out_shape=jax.ShapeDtypeStruct((n, d), F32),
        compiler_params=_params(("parallel",)),
        name="slab_out",
    )(o, h, w_out)


def _ffn_kernel(x_ref, g_ref, wa_ref, wb_ref, wo_ref, o_ref, xn_ref, acc_ref):
    k = pl.program_id(1)

    @pl.when(k == 0)
    def _():
        xn_ref[...] = (_rms(x_ref[...]) * g_ref[...]).astype(BF16)
        acc_ref[...] = jnp.zeros_like(acc_ref)

    xn = xn_ref[...]
    a = jnp.dot(xn, wa_ref[...], preferred_element_type=F32)
    b = jnp.dot(xn, wb_ref[...], preferred_element_type=F32)
    act = (a * _sigmoid(a) * b).astype(BF16)
    acc_ref[...] += jnp.dot(act, wo_ref[...], preferred_element_type=F32)

    @pl.when(k == pl.num_programs(1) - 1)
    def _():
        o_ref[...] = x_ref[...] + acc_ref[...]


def _ffn(h, gain, w_in, w_out):
    n, d = h.shape
    nk = FFN_HIDDEN // FFN_TK
    return pl.pallas_call(
        _ffn_kernel,
        grid=(n // FFN_TM, nk),
        in_specs=[
            pl.BlockSpec((FFN_TM, d), lambda i, k: (i, 0)),
            pl.BlockSpec((1, d), lambda i, k: (0, 0)),
            pl.BlockSpec((d, FFN_TK), lambda i, k: (0, k)),
            pl.BlockSpec((d, FFN_TK), lambda i, k: (0, nk + k)),
            pl.BlockSpec((FFN_TK, d), lambda i, k: (k, 0)),
        ],
        out_specs=pl.BlockSpec((FFN_TM, d), lambda i, k: (i, 0)),
        out_shape=jax.ShapeDtypeStruct((n, d), F32),
        scratch_shapes=[pltpu.VMEM((FFN_TM, d), BF16), pltpu.VMEM((FFN_TM, d), F32)],
        compiler_params=_params(("parallel", "arbitrary")),
        name="ffn",
    )(h, gain.reshape(1, d), w_in, w_in, w_out)


def _ple_kernel(h_ref, p_ref, gg_ref, wg_ref, wp_ref, pg_ref, o_ref):
    h = h_ref[...]
    hn = (_rms(h) * gg_ref[...]).astype(BF16)
    gate = _sigmoid(jnp.dot(hn, wg_ref[...], preferred_element_type=F32))
    e = jnp.dot(p_ref[...].astype(BF16), wp_ref[...], preferred_element_type=F32)
    o_ref[...] = h + gate * (_rms(e) * pg_ref[...])


def _ple(h, p, gate_gain, w_gate, w_proj, ple_gain):
    n, d = h.shape
    tm = OUT_TM
    row = lambda i: (i, 0)
    fixed = lambda i: (0, 0)
    return pl.pallas_call(
        _ple_kernel,
        grid=(n // tm,),
        in_specs=[pl.BlockSpec((tm, d), row), pl.BlockSpec((tm, p.shape[1]), row),
                  pl.BlockSpec((1, d), fixed), pl.BlockSpec(w_gate.shape, fixed),
                  pl.BlockSpec(w_proj.shape, fixed), pl.BlockSpec((1, d), fixed)],
        out_specs=pl.BlockSpec((tm, d), row),
        out_shape=jax.ShapeDtypeStruct((n, d), F32),
        compiler_params=_params(("parallel",)),
        name="ple",
    )(h, p, gate_gain.reshape(1, d), w_gate, w_proj, ple_gain.reshape(1, d))


def _diff_lambda_init(layer_idx):
    return 0.8 - 0.6 * math.exp(-0.3 * layer_idx)


def _tile_gains(rows):
    return jnp.stack([jnp.tile(r, LANES // r.shape[0]) for r in rows])[:, None, :].astype(F32)


def kernel(x, p, mix_norm, ffn_norm, a_w_in, a_w_out, b_w_in, b_q_norm, b_k_norm, b_w_out,
           c_w_in, c_q_norm, c_k_norm, c_lambda_q1, c_lambda_k1, c_lambda_q2, c_lambda_k2,
           c_subln, c_w_out, ffn_w_in, ffn_w_out, ple_w_proj, ple_norm, ple_gate_norm, ple_w_gate):
    batch, seq, d = x.shape
    depth = p.shape[0]
    n = batch * seq
    h = x.reshape(n, d)
    bf = lambda w: w.astype(BF16)
    for i in range(depth):
        kind, j = i % N_MIXERS, i // N_MIXERS
        if kind == 0:
            proj = _norm_proj(h, mix_norm[i], bf(a_w_in[j]))
            h = _retention(proj, h, bf(a_w_out[j]), batch, seq)
        elif kind == 1:
            ones = jnp.ones((LANES,), F32)
            gains = _tile_gains([b_q_norm[j]] * 3 + [b_k_norm[j]] * 3 + [ones])
            proj = _norm_proj(h, mix_norm[i], bf(b_w_in[j]), gains, "full", 6)
            outs = [_dilated_group(proj, gi, batch, seq) for gi in range(len(DIL_GROUPS))]
            h = _dilated_out([o for o, _ in outs], [l for _, l in outs], h, bf(b_w_out[j]))
        else:
            ones = jnp.ones((LANES,), F32)
            gains = _tile_gains([c_q_norm[j], c_k_norm[j], ones])
            proj = _norm_proj(h, mix_norm[i], bf(c_w_in[j]), gains, "half", 2)
            o = _diff_attention(proj, c_lambda_q1[j], c_lambda_k1[j], c_lambda_q2[j],
                                c_lambda_k2[j], c_subln[j], _diff_lambda_init(i), batch, seq)
            h = _slab_out(o, h, bf(c_w_out[j]))
        h = _ffn(h, ffn_norm[i], bf(ffn_w_in[i]), bf(ffn_w_out[i]))
        h = _ple(h, p[i].reshape(n, PLE_DIM), ple_gate_norm[i], bf(ple_w_gate[i]),
                 bf(ple_w_proj[i]), ple_norm[i])
    return h.reshape(batch, seq, d)
```

```python
import functools
import math

import numpy as np
import jax
import jax.numpy as jnp
from jax import lax
from jax.experimental import pallas as pl
from jax.experimental.pallas import tpu as pltpu

D_MODEL = 1024
PLE_DIM = 256
N_MIXERS = 3
RMS_EPS = 1e-6
NEG = -1e30

RET_HEADS = 4
RET_DK = 256
RET_DV = 512
RET_CHUNK = 128
RET_IN = 2 * RET_HEADS * RET_DK + 2 * RET_HEADS * RET_DV

DIL_GROUPS = ((128, 1), (512, 4), (2048, 16))
DIL_HEADS = 8
DIL_HEAD_DIM = 128
DIL_BLOCK = 128
DIL_QK = len(DIL_GROUPS) * DIL_HEADS * DIL_HEAD_DIM
DIL_IN = 2 * DIL_QK + DIL_HEADS * DIL_HEAD_DIM

DIFF_HEADS = 8
DIFF_HEAD_DIM = 64
DIFF_QK = 2 * DIFF_HEADS * DIFF_HEAD_DIM
DIFF_V = DIFF_HEADS * 2 * DIFF_HEAD_DIM
DIFF_IN = 2 * DIFF_QK + DIFF_V

FFN_HIDDEN = 2816

LANES = 128
V7X_VMEM_BYTES = 64 * 1024 * 1024
VMEM_LIMIT_BYTES = V7X_VMEM_BYTES - 8 * 1024 * 1024

PROJ_TM = 512
PROJ_TN = 1024
FFN_TM = 512
FFN_TK = 1408
RET_T = 512
DIL_WINDOW = 2048
DIFF_TQ = 1024
DIFF_TK = 1024
OUT_TM = 512

LOG2E = math.log2(math.e)

F32 = jnp.float32
BF16 = jnp.bfloat16

_NT = (((1,), (1,)), ((), ()))
_TN = (((0,), (0,)), ((), ()))


def _params(semantics):
    return pltpu.CompilerParams(dimension_semantics=semantics, vmem_limit_bytes=VMEM_LIMIT_BYTES)


def _rms(x):
    return x * lax.rsqrt(jnp.mean(x * x, axis=-1, keepdims=True) + RMS_EPS)


def _sigmoid(x):
    return 1.0 / (1.0 + jnp.exp(-x))


def _norm_proj_kernel(x_ref, g_ref, w_ref, hg_ref, o_ref, xn_ref, *, head_norm, n_norm_tiles):
    j = pl.program_id(1)

    @pl.when(j == 0)
    def _():
        xn_ref[...] = (_rms(x_ref[...]) * g_ref[...]).astype(BF16)

    y = jnp.dot(xn_ref[...], w_ref[...], preferred_element_type=F32)
    slabs = y.shape[1] // LANES

    def store(fn):
        for s in range(slabs):
            o_ref[s] = fn(y[:, s * LANES:(s + 1) * LANES]).astype(o_ref.dtype)

    def full_norm(z):
        return _rms(z) * hg_ref[...]

    def half_norm(z):
        lo = lax.broadcasted_iota(jnp.int32, z.shape, 1) < (LANES // 2)
        zz = z * z
        ss_lo = jnp.sum(jnp.where(lo, zz, 0.0), axis=-1, keepdims=True)
        ss_hi = jnp.sum(jnp.where(lo, 0.0, zz), axis=-1, keepdims=True)
        inv = jnp.where(lo, lax.rsqrt(ss_lo / (LANES // 2) + RMS_EPS),
                        lax.rsqrt(ss_hi / (LANES // 2) + RMS_EPS))
        return z * inv * hg_ref[...]

    if head_norm is None:
        store(lambda z: z)
    else:
        fn = full_norm if head_norm == "full" else half_norm
        pl.when(j < n_norm_tiles)(lambda: store(fn))
        pl.when(j >= n_norm_tiles)(lambda: store(lambda z: z))


def _norm_proj(h, gain, w, head_gains=None, head_norm=None, n_norm_tiles=0, out_dtype=BF16):
    n, d = h.shape
    n_out = w.shape[1]
    n_tiles = n_out // PROJ_TN
    slabs = PROJ_TN // LANES
    if head_gains is None:
        head_gains = jnp.ones((n_tiles, 1, LANES), F32)
    kern = functools.partial(_norm_proj_kernel, head_norm=head_norm, n_norm_tiles=n_norm_tiles)
    return pl.pallas_call(
        kern,
        grid=(n // PROJ_TM, n_tiles),
        in_specs=[
            pl.BlockSpec((PROJ_TM, d), lambda i, j: (i, 0)),
            pl.BlockSpec((1, d), lambda i, j: (0, 0)),
            pl.BlockSpec((d, PROJ_TN), lambda i, j: (0, j)),
            pl.BlockSpec((None, 1, LANES), lambda i, j: (j, 0, 0)),
        ],
        out_specs=pl.BlockSpec((slabs, PROJ_TM, LANES), lambda i, j: (j, i, 0)),
        out_shape=jax.ShapeDtypeStruct((n_out // LANES, n, LANES), out_dtype),
        scratch_shapes=[pltpu.VMEM((PROJ_TM, d), BF16)],
        compiler_params=_params(("parallel", "arbitrary")),
        name="norm_proj",
    )(h, gain.reshape(1, d), w, head_gains)


def _retention_tables():
    h = np.arange(RET_HEADS, dtype=np.float32)
    log_g = np.log(np.float32(1.0) - np.float32(2.0) ** (np.float32(-5.0) - h)).astype(np.float32)
    pos = np.arange(RET_CHUNK, dtype=np.float32)
    rel = pos[:, None] - pos[None, :]
    scale = np.float32(RET_DK ** -0.5)
    din = np.where(rel >= 0, np.exp(np.maximum(rel, 0.0)[None] * log_g[:, None, None]), 0.0)
    dq = np.exp((pos + 1.0)[None] * log_g[:, None])
    dk = np.exp((RET_CHUNK - 1.0 - pos)[None] * log_g[:, None])
    dchunk = np.exp(RET_CHUNK * log_g)
    return ((din * scale).astype(np.float32), dq.astype(np.float32)[:, :, None],
            (dk * scale).astype(np.float32)[:, :, None], tuple(float(v) for v in dchunk))


def _retention_kernel(q_ref, k_ref, v_ref, g_ref, h_ref, wo_ref, din_ref, dq_ref, dk_ref,
                      o_ref, r_ref, y_ref, *, dchunk):
    @pl.when(pl.program_id(1) == 0)
    def _():
        r_ref[...] = jnp.zeros_like(r_ref)

    qs, vs = RET_DK // LANES, RET_DV // LANES

    def cat(ref, first, count, rows):
        return jnp.concatenate([ref[first + s, rows, :] for s in range(count)], axis=-1)

    def chunk(c, carry):
        rows = pl.ds(pl.multiple_of(c * RET_CHUNK, RET_CHUNK), RET_CHUNK)
        for hh in range(RET_HEADS):
            qc = cat(q_ref, hh * qs, qs, rows)
            kc = cat(k_ref, hh * qs, qs, rows)
            vc = cat(v_ref, hh * vs, vs, rows)
            gc = cat(g_ref, hh * vs, vs, rows).astype(F32)
            att = lax.dot_general(qc, kc, _NT, preferred_element_type=F32) * din_ref[hh]
            state = r_ref[hh]
            y = (jnp.dot(att.astype(BF16), vc, preferred_element_type=F32)
                 + jnp.dot(qc, state.astype(BF16), preferred_element_type=F32) * dq_ref[hh])
            kd = (kc.astype(F32) * dk_ref[hh]).astype(BF16)
            r_ref[hh] = state * dchunk[hh] + lax.dot_general(kd, vc, _TN,
                                                             preferred_element_type=F32)
            y_ref[rows, hh * RET_DV:(hh + 1) * RET_DV] = (gc * _sigmoid(gc) * _rms(y)).astype(BF16)
        return carry

    lax.fori_loop(0, y_ref.shape[0] // RET_CHUNK, chunk, 0)
    o_ref[...] = h_ref[...] + jnp.dot(y_ref[...], wo_ref[...], preferred_element_type=F32)


def _retention(proj, h, w_out, batch, seq):
    n, d = h.shape
    din, dq, dk, dchunk = _retention_tables()
    p4 = proj.reshape(proj.shape[0], batch, seq, LANES)
    h3 = h.reshape(batch, seq, d)
    nq = RET_HEADS * RET_DK // LANES
    nv = RET_HEADS * RET_DV // LANES
    t = RET_T
    slab_spec = lambda cnt, blk: pl.BlockSpec((cnt, None, t, LANES), lambda b, i: (blk, b, i, 0))
    const3 = lambda a: pl.BlockSpec(a.shape, lambda b, i: (0, 0, 0))
    out = pl.pallas_call(
        functools.partial(_retention_kernel, dchunk=dchunk),
        grid=(batch, seq // t),
        in_specs=[
            slab_spec(nq, 0), slab_spec(nq, 1), slab_spec(nv, 1), slab_spec(nv, 2),
            pl.BlockSpec((None, t, d), lambda b, i: (b, i, 0)),
            pl.BlockSpec(w_out.shape, lambda b, i: (0, 0)),
            const3(din), const3(dq), const3(dk),
        ],
        out_specs=pl.BlockSpec((None, t, d), lambda b, i: (b, i, 0)),
        out_shape=jax.ShapeDtypeStruct((batch, seq, d), F32),
        scratch_shapes=[pltpu.VMEM((RET_HEADS, RET_DK, RET_DV), F32),
                        pltpu.VMEM((t, RET_HEADS * RET_DV), BF16)],
        compiler_params=_params(("parallel", "arbitrary")),
        name="retention",
    )(p4, p4, p4, p4, h3, w_out, jnp.asarray(din), jnp.asarray(dq), jnp.asarray(dk))
    return out.reshape(n, d)


def _alibi_slopes(n):
    ratio = 2.0 ** (-8.0 / n)
    return np.array([ratio ** (i + 1) for i in range(n)], dtype=np.float32)


def _dilated_tables():
    slopes = _alibi_slopes(DIL_HEADS)
    rel = (DIL_BLOCK + np.arange(DIL_BLOCK))[:, None] - np.arange(2 * DIL_BLOCK)[None, :]
    tabs = []
    for window, dilation in DIL_GROUPS:
        valid = (rel >= 0) & (rel <= window // dilation)
        bias = -(slopes[:, None, None] * (rel * dilation).astype(np.float32)[None])
        tabs.append(np.where(valid[None], bias, np.float32(NEG)).astype(np.float32))
    return np.stack(tabs, axis=1)


def _dilated_kernel(q0, q1, q2, k0, k1, k2, kh0, kh1, kh2, v_ref, vh_ref, tab_ref, o_ref,
                    og_ref, lg_ref):
    scale = DIL_HEAD_DIM ** -0.5
    first_window = pl.program_id(2) == 0
    in_prev_block = lax.broadcasted_iota(jnp.int32, (DIL_BLOCK, 2 * DIL_BLOCK), 1) < DIL_BLOCK
    ones = jnp.ones((2 * DIL_BLOCK, LANES), BF16)
    for g, (q_ref, k_ref, kh_ref) in enumerate(((q0, k0, kh0), (q1, k1, kh1), (q2, k2, kh2))):
        dilation = DIL_GROUPS[g][1]
        span = DIL_BLOCK * dilation
        n_blocks = DIL_WINDOW // span
        tab = tab_ref[g]
        tab_first = jnp.where(jnp.logical_and(first_window, in_prev_block), NEG, tab)

        def rows(start):
            if dilation == 1:
                return pl.ds(start, DIL_BLOCK)
            return pl.ds(start, DIL_BLOCK, stride=dilation)

        for r in range(dilation):
            kb = [kh_ref[rows(r), :].astype(BF16)]
            vb = [vh_ref[rows(DIL_WINDOW - span + r), :].astype(BF16)]
            for blk in range(n_blocks):
                kb.append(k_ref[rows(r + blk * span), :].astype(BF16))
                vb.append(v_ref[rows(r + blk * span), :].astype(BF16))
            scores = []
            for blk in range(n_blocks):
                qb = q_ref[rows(r + blk * span), :].astype(BF16)
                keys = jnp.concatenate([kb[blk], kb[blk + 1]], axis=0)
                s = lax.dot_general(qb, keys, _NT, preferred_element_type=F32) * scale
                scores.append(s + (tab_first if blk == 0 else tab))
            for blk in range(n_blocks):
                s = scores[blk]
                m = jnp.max(s, axis=-1, keepdims=True)
                p = jnp.exp(s - m).astype(BF16)
                vals = jnp.concatenate([jnp.concatenate([vb[blk], vb[blk + 1]], axis=0), ones], axis=1)
                res = jnp.dot(p, vals, preferred_element_type=F32)
                den = res[:, LANES:]
                og_ref[g, rows(r + blk * span), :] = res[:, :LANES] / den
                lg_ref[g, rows(r + blk * span), :] = m + jnp.log(den)
    l0, l1, l2 = lg_ref[0], lg_ref[1], lg_ref[2]
    m = jnp.maximum(jnp.maximum(l0, l1), l2)
    e0, e1, e2 = jnp.exp(l0 - m), jnp.exp(l1 - m), jnp.exp(l2 - m)
    mix = e0 * og_ref[0] + e1 * og_ref[1] + e2 * og_ref[2]
    o_ref[...] = (mix / (e0 + e1 + e2)).astype(o_ref.dtype)


def _dilated_attention(proj, batch, seq):
    n = batch * seq
    p4 = proj.reshape(proj.shape[0], batch, seq, LANES)
    n_groups = len(DIL_GROUPS)
    v0 = 2 * n_groups * DIL_HEADS

    def cur(first):
        return pl.BlockSpec((None, None, DIL_WINDOW, LANES), lambda b, hh, w: (first + hh, b, w, 0))

    def halo(first, rows):
        per_window = DIL_WINDOW // rows
        return pl.BlockSpec((None, None, rows, LANES),
                            lambda b, hh, w: (first + hh, b, jnp.maximum(w * per_window - 1, 0), 0))

    q_specs = [cur(g * DIL_HEADS) for g in range(n_groups)]
    k_specs = [cur((n_groups + g) * DIL_HEADS) for g in range(n_groups)]
    kh_specs = [halo((n_groups + g) * DIL_HEADS, DIL_BLOCK * DIL_GROUPS[g][1]) for g in range(n_groups)]
    tabs = _dilated_tables()
    tab_spec = pl.BlockSpec((None,) + tabs.shape[1:], lambda b, hh, w: (hh, 0, 0, 0))
    o = pl.pallas_call(
        _dilated_kernel,
        grid=(batch, DIL_HEADS, seq // DIL_WINDOW),
        in_specs=q_specs + k_specs + kh_specs + [cur(v0), halo(v0, DIL_WINDOW), tab_spec],
        out_specs=pl.BlockSpec((None, None, DIL_WINDOW, LANES), lambda b, hh, w: (hh, b, w, 0)),
        out_shape=jax.ShapeDtypeStruct((DIL_HEADS, batch, seq, LANES), BF16),
        scratch_shapes=[pltpu.VMEM((n_groups, DIL_WINDOW, LANES), F32),
                        pltpu.VMEM((n_groups, DIL_WINDOW, LANES), F32)],
        compiler_params=_params(("parallel", "parallel", "arbitrary")),
        name="dilated_attention",
    )(*([p4] * 11), jnp.asarray(tabs))
    return o.reshape(DIL_HEADS, n, LANES)


def _diff_steps(seq):
    ratio = DIFF_TQ // DIFF_TK
    qi, ki = [], []
    for a in range(seq // DIFF_TQ):
        for b in range(ratio * (a + 1)):
            qi.append(a)
            ki.append(b)
    return np.asarray(qi, np.int32), np.asarray(ki, np.int32)


def _diff_kernel(qi_tab, ki_tab, q_ref, k_ref, v_ref, slope_ref, lq1, lk1, lq2, lk2, sg_ref,
                 o_ref, tab_ref, m1, l1, a1, m2, l2, a2, *, lambda_init):
    step = pl.program_id(2)
    qi = qi_tab[step]
    ki = ki_tab[step]
    ratio = DIFF_TQ // DIFF_TK
    slope = slope_ref[...]

    @pl.when(step == 0)
    def _():
        key = lax.broadcasted_iota(jnp.int32, (DIFF_TK, DIFF_TQ), 0)
        qry = lax.broadcasted_iota(jnp.int32, (DIFF_TK, DIFF_TQ), 1)
        pen = slope * (qry - key).astype(F32)
        tab_ref[0] = pen
        for j in range(ratio):
            tab_ref[1 + j] = jnp.where(qry - key - j * DIFF_TK >= 0, pen, -NEG)

    @pl.when(ki == 0)
    def _():
        for m, l, a in ((m1, l1, a1), (m2, l2, a2)):
            m[...] = jnp.full_like(m, NEG)
            l[...] = jnp.zeros_like(l)
            a[...] = jnp.zeros_like(a)

    diag = ki - ratio * qi
    tab = tab_ref[jnp.where(diag < 0, 0, diag + 1)]
    offset = slope * (qi * DIFF_TQ - ki * DIFF_TK).astype(F32)

    q = q_ref[...]
    lo = lax.broadcasted_iota(jnp.int32, q.shape, 1) < DIFF_HEAD_DIM
    zero = jnp.zeros_like(q)
    k = k_ref[...]
    v = v_ref[...]
    for qz, m_ref, l_ref, a_ref in ((jnp.where(lo, q, zero), m1, l1, a1),
                                    (jnp.where(lo, zero, q), m2, l2, a2)):
        t = lax.dot_general(k, qz, _NT, preferred_element_type=F32) - tab
        m_old = m_ref[...]
        m_new = jnp.maximum(m_old, jnp.max(t, axis=0, keepdims=True) - offset)
        alpha = jnp.exp2(m_old - m_new)
        p = jnp.exp2(t - (m_new + offset))
        l_ref[...] = alpha * l_ref[...] + jnp.sum(p, axis=0, keepdims=True)
        a_ref[...] = a_ref[...] * alpha + lax.dot_general(v, p.astype(BF16), _TN,
                                                          preferred_element_type=F32)
        m_ref[...] = m_new

    @pl.when(diag == ratio - 1)
    def _():
        lam = (jnp.exp(jnp.sum(lq1[...] * lk1[...], axis=-1, keepdims=True))
               - jnp.exp(jnp.sum(lq2[...] * lk2[...], axis=-1, keepdims=True)) + lambda_init)
        o_t = a1[...] / l1[...] - lam * (a2[...] / l2[...])
        o = _rms(o_t.T) * sg_ref[...] * (1.0 - lambda_init)
        o_ref[...] = o.astype(o_ref.dtype)


def _diff_attention(proj, lq1, lk1, lq2, lk2, subln, lambda_init, batch, seq):
    n = batch * seq
    p4 = proj.reshape(proj.shape[0], batch, seq, LANES)
    qi_tab, ki_tab = _diff_steps(seq)
    slopes = np.repeat((_alibi_slopes(DIFF_HEADS) * np.float32(LOG2E))[:, None, None], DIFF_TQ, axis=2)
    vec = lambda a: a.reshape(1, -1).astype(F32)
    small = lambda a: pl.BlockSpec(a.shape, lambda b, hh, s, qt, kt: (0, 0))
    lqs = [vec(lq1), vec(lk1), vec(lq2), vec(lk2)]
    sg = vec(subln)
    grid_spec = pltpu.PrefetchScalarGridSpec(
        num_scalar_prefetch=2,
        grid=(batch, DIFF_HEADS, len(qi_tab)),
        in_specs=[
            pl.BlockSpec((None, None, DIFF_TQ, LANES), lambda b, hh, s, qt, kt: (hh, b, qt[s], 0)),
            pl.BlockSpec((None, None, DIFF_TK, LANES),
                         lambda b, hh, s, qt, kt: (DIFF_HEADS + hh, b, kt[s], 0)),
            pl.BlockSpec((None, None, DIFF_TK, LANES),
                         lambda b, hh, s, qt, kt: (2 * DIFF_HEADS + hh, b, kt[s], 0)),
            pl.BlockSpec((None, 1, DIFF_TQ), lambda b, hh, s, qt, kt: (hh, 0, 0)),
            small(lqs[0]), small(lqs[1]), small(lqs[2]), small(lqs[3]), small(sg),
        ],
        out_specs=pl.BlockSpec((None, None, DIFF_TQ, LANES), lambda b, hh, s, qt, kt: (hh, b, qt[s], 0)),
        scratch_shapes=[
            pltpu.VMEM((1 + DIFF_TQ // DIFF_TK, DIFF_TK, DIFF_TQ), F32),
            pltpu.VMEM((1, DIFF_TQ), F32), pltpu.VMEM((1, DIFF_TQ), F32),
            pltpu.VMEM((2 * DIFF_HEAD_DIM, DIFF_TQ), F32),
            pltpu.VMEM((1, DIFF_TQ), F32), pltpu.VMEM((1, DIFF_TQ), F32),
            pltpu.VMEM((2 * DIFF_HEAD_DIM, DIFF_TQ), F32),
        ],
    )
    o = pl.pallas_call(
        functools.partial(_diff_kernel, lambda_init=lambda_init),
        grid_spec=grid_spec,
        out_shape=jax.ShapeDtypeStruct((DIFF_HEADS, batch, seq, LANES), BF16),
        compiler_params=_params(("parallel", "parallel", "arbitrary")),
        name="diff_attention",
    )(jnp.asarray(qi_tab), jnp.asarray(ki_tab), p4, p4, p4, jnp.asarray(slopes), *lqs, sg)
    return o.reshape(DIFF_HEADS, n, LANES)


def _slab_out_kernel(o_ref, h_ref, w_ref, out_ref):
    o = jnp.concatenate([o_ref[s] for s in range(o_ref.shape[0])], axis=-1)
    out_ref[...] = h_ref[...] + jnp.dot(o, w_ref[...], preferred_element_type=F32)


def _slab_out(o, h, w_out):
    n, d = h.shape
    tm = OUT_TM
    return pl.pallas_call(
        _slab_out_kernel,
        grid=(n // tm,),
        in_specs=[pl.BlockSpec((o.shape[0], tm, LANES), lambda i: (0, i, 0)),
                  pl.BlockSpec((tm, d), lambda i: (i, 0)),
                  pl.BlockSpec(w_out.shape, lambda i: (0, 0))],
        out_specs=pl.BlockSpec((tm, d), lambda i: (i, 0)),
        out_shape=jax.ShapeDtypeStruct((n, d), F32),
        compiler_params=_params(("parallel",)),
        name="slab_out",
    )(o, h, w_out)


def _ffn_kernel(x_ref, g_ref, wa_ref, wb_ref, wo_ref, o_ref, xn_ref, acc_ref):
    k = pl.program_id(1)

    @pl.when(k == 0)
    def _():
        xn_ref[...] = (_rms(x_ref[...]) * g_ref[...]).astype(BF16)
        acc_ref[...] = jnp.zeros_like(acc_ref)

    xn = xn_ref[...]
    a = jnp.dot(xn, wa_ref[...], preferred_element_type=F32)
    b = jnp.dot(xn, wb_ref[...], preferred_element_type=F32)
    act = (a * _sigmoid(a) * b).astype(BF16)
    acc_ref[...] += jnp.dot(act, wo_ref[...], preferred_element_type=F32)

    @pl.when(k == pl.num_programs(1) - 1)
    def _():
        o_ref[...] = x_ref[...] + acc_ref[...]


def _ffn(h, gain, w_in, w_out):
    n, d = h.shape
    nk = FFN_HIDDEN // FFN_TK
    return pl.pallas_call(
        _ffn_kernel,
        grid=(n // FFN_TM, nk),
        in_specs=[
            pl.BlockSpec((FFN_TM, d), lambda i, k: (i, 0)),
            pl.BlockSpec((1, d), lambda i, k: (0, 0)),
            pl.BlockSpec((d, FFN_TK), lambda i, k: (0, k)),
            pl.BlockSpec((d, FFN_TK), lambda i, k: (0, nk + k)),
            pl.BlockSpec((FFN_TK, d), lambda i, k: (k, 0)),
        ],
        out_specs=pl.BlockSpec((FFN_TM, d), lambda i, k: (i, 0)),
        out_shape=jax.ShapeDtypeStruct((n, d), F32),
        scratch_shapes=[pltpu.VMEM((FFN_TM, d), BF16), pltpu.VMEM((FFN_TM, d), F32)],
        compiler_params=_params(("parallel", "arbitrary")),
        name="ffn",
    )(h, gain.reshape(1, d), w_in, w_in, w_out)


def _ple_kernel(h_ref, p_ref, gg_ref, wg_ref, wp_ref, pg_ref, o_ref):
    h = h_ref[...]
    hn = (_rms(h) * gg_ref[...]).astype(BF16)
    gate = _sigmoid(jnp.dot(hn, wg_ref[...], preferred_element_type=F32))
    e = jnp.dot(p_ref[...].astype(BF16), wp_ref[...], preferred_element_type=F32)
    o_ref[...] = h + gate * (_rms(e) * pg_ref[...])


def _ple(h, p, gate_gain, w_gate, w_proj, ple_gain):
    n, d = h.shape
    tm = OUT_TM
    row = lambda i: (i, 0)
    fixed = lambda i: (0, 0)
    return pl.pallas_call(
        _ple_kernel,
        grid=(n // tm,),
        in_specs=[pl.BlockSpec((tm, d), row), pl.BlockSpec((tm, p.shape[1]), row),
                  pl.BlockSpec((1, d), fixed), pl.BlockSpec(w_gate.shape, fixed),
                  pl.BlockSpec(w_proj.shape, fixed), pl.BlockSpec((1, d), fixed)],
        out_specs=pl.BlockSpec((tm, d), row),
        out_shape=jax.ShapeDtypeStruct((n, d), F32),
        compiler_params=_params(("parallel",)),
        name="ple",
    )(h, p, gate_gain.reshape(1, d), w_gate, w_proj, ple_gain.reshape(1, d))


def _diff_lambda_init(layer_idx):
    return 0.8 - 0.6 * math.exp(-0.3 * layer_idx)


def _tile_gains(rows):
    return jnp.stack([jnp.tile(r, LANES // r.shape[0]) for r in rows])[:, None, :].astype(F32)


def kernel(x, p, mix_norm, ffn_norm, a_w_in, a_w_out, b_w_in, b_q_norm, b_k_norm, b_w_out,
           c_w_in, c_q_norm, c_k_norm, c_lambda_q1, c_lambda_k1, c_lambda_q2, c_lambda_k2,
           c_subln, c_w_out, ffn_w_in, ffn_w_out, ple_w_proj, ple_norm, ple_gate_norm, ple_w_gate):
    batch, seq, d = x.shape
    depth = p.shape[0]
    n = batch * seq
    h = x.reshape(n, d)
    bf = lambda w: w.astype(BF16)
    for i in range(depth):
        kind, j = i % N_MIXERS, i // N_MIXERS
        if kind == 0:
            proj = _norm_proj(h, mix_norm[i], bf(a_w_in[j]))
            h = _retention(proj, h, bf(a_w_out[j]), batch, seq)
        elif kind == 1:
            ones = jnp.ones((LANES,), F32)
            gains = _tile_gains([b_q_norm[j]] * 3 + [b_k_norm[j]] * 3 + [ones])
            proj = _norm_proj(h, mix_norm[i], bf(b_w_in[j]), gains, "full", 6, out_dtype=F32)
            h = _slab_out(_dilated_attention(proj, batch, seq), h, bf(b_w_out[j]))
        else:
            ones = jnp.ones((LANES,), F32)
            gains = _tile_gains([c_q_norm[j] * (DIFF_HEAD_DIM ** -0.5 * LOG2E), c_k_norm[j], ones])
            proj = _norm_proj(h, mix_norm[i], bf(c_w_in[j]), gains, "half", 2)
            o = _diff_attention(proj, c_lambda_q1[j], c_lambda_k1[j], c_lambda_q2[j],
                                c_lambda_k2[j], c_subln[j], _diff_lambda_init(i), batch, seq)
            h = _slab_out(o, h, bf(c_w_out[j]))
        h = _ffn(h, ffn_norm[i], bf(ffn_w_in[i]), bf(ffn_w_out[i]))
        h = _ple(h, p[i].reshape(n, PLE_DIM), ple_gate_norm[i], bf(ple_w_gate[i]),
                 bf(ple_w_proj[i]), ple_norm[i])
    return h.reshape(batch, seq, d)
```

```python
import functools
import math

import numpy as np
import jax
import jax.numpy as jnp
from jax import lax
from jax.experimental import pallas as pl
from jax.experimental.pallas import tpu as pltpu

D_MODEL = 1024
PLE_DIM = 256
N_MIXERS = 3
RMS_EPS = 1e-6
NEG = -1e30

RET_HEADS = 4
RET_DK = 256
RET_DV = 512
RET_CHUNK = 128
RET_IN = 2 * RET_HEADS * RET_DK + 2 * RET_HEADS * RET_DV

DIL_GROUPS = ((128, 1), (512, 4), (2048, 16))
DIL_HEADS = 8
DIL_HEAD_DIM = 128
DIL_BLOCK = 128
DIL_QK = len(DIL_GROUPS) * DIL_HEADS * DIL_HEAD_DIM
DIL_IN = 2 * DIL_QK + DIL_HEADS * DIL_HEAD_DIM

DIFF_HEADS = 8
DIFF_HEAD_DIM = 64
DIFF_QK = 2 * DIFF_HEADS * DIFF_HEAD_DIM
DIFF_V = DIFF_HEADS * 2 * DIFF_HEAD_DIM
DIFF_IN = 2 * DIFF_QK + DIFF_V

FFN_HIDDEN = 2816

LANES = 128
V7X_VMEM_BYTES = 64 * 1024 * 1024
VMEM_LIMIT_BYTES = V7X_VMEM_BYTES - 8 * 1024 * 1024

PROJ_TM = 1024
PROJ_TN = 1024
FFN_TM = 512
FFN_CHUNK = 1024
RET_T = 512
DIL_WINDOW = 2048
DIFF_T = 1024
DIFF_STRIP = 256
DIFF_KEY_BLOCK = 256
OUT_TM = 512

LOG2E = math.log2(math.e)

F32 = jnp.float32
BF16 = jnp.bfloat16

_NT = (((1,), (1,)), ((), ()))
_TN = (((0,), (0,)), ((), ()))


def _params(semantics):
    return pltpu.CompilerParams(dimension_semantics=semantics, vmem_limit_bytes=VMEM_LIMIT_BYTES)


def _rms(x):
    return x * lax.rsqrt(jnp.mean(x * x, axis=-1, keepdims=True) + RMS_EPS)


def _sigmoid(x):
    return 1.0 / (1.0 + jnp.exp(-x))


def _norm_proj_kernel(x_ref, g_ref, w_ref, hg_ref, o_ref, xn_ref, *, head_norm, n_norm_tiles):
    j = pl.program_id(1)

    @pl.when(j == 0)
    def _():
        xn_ref[...] = (_rms(x_ref[...]) * g_ref[...]).astype(BF16)

    y = jnp.dot(xn_ref[...], w_ref[...], preferred_element_type=F32)
    slabs = y.shape[1] // LANES

    def store(fn):
        for s in range(slabs):
            o_ref[s] = fn(y[:, s * LANES:(s + 1) * LANES]).astype(o_ref.dtype)

    def full_norm(z):
        return _rms(z) * hg_ref[...]

    def half_norm(z):
        lo = lax.broadcasted_iota(jnp.int32, z.shape, 1) < (LANES // 2)
        zz = z * z
        ss_lo = jnp.sum(jnp.where(lo, zz, 0.0), axis=-1, keepdims=True)
        ss_hi = jnp.sum(jnp.where(lo, 0.0, zz), axis=-1, keepdims=True)
        inv = jnp.where(lo, lax.rsqrt(ss_lo / (LANES // 2) + RMS_EPS),
                        lax.rsqrt(ss_hi / (LANES // 2) + RMS_EPS))
        return z * inv * hg_ref[...]

    if head_norm is None:
        store(lambda z: z)
    else:
        fn = full_norm if head_norm == "full" else half_norm
        pl.when(j < n_norm_tiles)(lambda: store(fn))
        pl.when(j >= n_norm_tiles)(lambda: store(lambda z: z))


def _norm_proj(h, gain, w, head_gains=None, head_norm=None, n_norm_tiles=0, out_dtype=BF16):
    n, d = h.shape
    n_out = w.shape[1]
    n_tiles = n_out // PROJ_TN
    slabs = PROJ_TN // LANES
    if head_gains is None:
        head_gains = jnp.ones((n_tiles, 1, LANES), F32)
    kern = functools.partial(_norm_proj_kernel, head_norm=head_norm, n_norm_tiles=n_norm_tiles)
    return pl.pallas_call(
        kern,
        grid=(n // PROJ_TM, n_tiles),
        in_specs=[
            pl.BlockSpec((PROJ_TM, d), lambda i, j: (i, 0)),
            pl.BlockSpec((1, d), lambda i, j: (0, 0)),
            pl.BlockSpec((d, PROJ_TN), lambda i, j: (0, j)),
            pl.BlockSpec((None, 1, LANES), lambda i, j: (j, 0, 0)),
        ],
        out_specs=pl.BlockSpec((slabs, PROJ_TM, LANES), lambda i, j: (j, i, 0)),
        out_shape=jax.ShapeDtypeStruct((n_out // LANES, n, LANES), out_dtype),
        scratch_shapes=[pltpu.VMEM((PROJ_TM, d), BF16)],
        compiler_params=_params(("parallel", "arbitrary")),
        name="norm_proj",
    )(h, gain.reshape(1, d), w, head_gains)


def _retention_tables():
    h = np.arange(RET_HEADS, dtype=np.float32)
    log_g = np.log(np.float32(1.0) - np.float32(2.0) ** (np.float32(-5.0) - h)).astype(np.float32)
    pos = np.arange(RET_CHUNK, dtype=np.float32)
    rel = pos[:, None] - pos[None, :]
    scale = np.float32(RET_DK ** -0.5)
    din = np.where(rel >= 0, np.exp(np.maximum(rel, 0.0)[None] * log_g[:, None, None]), 0.0)
    dq = np.exp((pos + 1.0)[None] * log_g[:, None])
    dk = np.exp((RET_CHUNK - 1.0 - pos)[None] * log_g[:, None])
    dchunk = np.exp(RET_CHUNK * log_g)
    return ((din * scale).astype(np.float32), dq.astype(np.float32)[:, :, None],
            (dk * scale).astype(np.float32)[:, :, None], tuple(float(v) for v in dchunk))


def _retention_kernel(q_ref, k_ref, v_ref, g_ref, h_ref, wo_ref, din_ref, dq_ref, dk_ref,
                      o_ref, r_ref, y_ref, *, dchunk):
    @pl.when(pl.program_id(1) == 0)
    def _():
        r_ref[...] = jnp.zeros_like(r_ref)

    qs, vs = RET_DK // LANES, RET_DV // LANES

    def cat(ref, first, count, rows):
        return jnp.concatenate([ref[first + s, rows, :] for s in range(count)], axis=-1)

    def chunk(c, carry):
        rows = pl.ds(pl.multiple_of(c * RET_CHUNK, RET_CHUNK), RET_CHUNK)
        for hh in range(RET_HEADS):
            qc = cat(q_ref, hh * qs, qs, rows)
            kc = cat(k_ref, hh * qs, qs, rows)
            vc = cat(v_ref, hh * vs, vs, rows)
            gc = cat(g_ref, hh * vs, vs, rows).astype(F32)
            att = lax.dot_general(qc, kc, _NT, preferred_element_type=F32) * din_ref[hh]
            state = r_ref[hh]
            y = (jnp.dot(att.astype(BF16), vc, preferred_element_type=F32)
                 + jnp.dot(qc, state.astype(BF16), preferred_element_type=F32) * dq_ref[hh])
            kd = (kc.astype(F32) * dk_ref[hh]).astype(BF16)
            r_ref[hh] = state * dchunk[hh] + lax.dot_general(kd, vc, _TN,
                                                             preferred_element_type=F32)
            y_ref[rows, hh * RET_DV:(hh + 1) * RET_DV] = (gc * _sigmoid(gc) * _rms(y)).astype(BF16)
        return carry

    lax.fori_loop(0, y_ref.shape[0] // RET_CHUNK, chunk, 0)
    o_ref[...] = h_ref[...] + jnp.dot(y_ref[...], wo_ref[...], preferred_element_type=F32)


def _retention(proj, h, w_out, batch, seq):
    n, d = h.shape
    din, dq, dk, dchunk = _retention_tables()
    p4 = proj.reshape(proj.shape[0], batch, seq, LANES)
    h3 = h.reshape(batch, seq, d)
    nq = RET_HEADS * RET_DK // LANES
    nv = RET_HEADS * RET_DV // LANES
    t = RET_T
    slab_spec = lambda cnt, blk: pl.BlockSpec((cnt, None, t, LANES), lambda b, i: (blk, b, i, 0))
    const3 = lambda a: pl.BlockSpec(a.shape, lambda b, i: (0, 0, 0))
    out = pl.pallas_call(
        functools.partial(_retention_kernel, dchunk=dchunk),
        grid=(batch, seq // t),
        in_specs=[
            slab_spec(nq, 0), slab_spec(nq, 1), slab_spec(nv, 1), slab_spec(nv, 2),
            pl.BlockSpec((None, t, d), lambda b, i: (b, i, 0)),
            pl.BlockSpec(w_out.shape, lambda b, i: (0, 0)),
            const3(din), const3(dq), const3(dk),
        ],
        out_specs=pl.BlockSpec((None, t, d), lambda b, i: (b, i, 0)),
        out_shape=jax.ShapeDtypeStruct((batch, seq, d), F32),
        scratch_shapes=[pltpu.VMEM((RET_HEADS, RET_DK, RET_DV), F32),
                        pltpu.VMEM((t, RET_HEADS * RET_DV), BF16)],
        compiler_params=_params(("parallel", "arbitrary")),
        name="retention",
    )(p4, p4, p4, p4, h3, w_out, jnp.asarray(din), jnp.asarray(dq), jnp.asarray(dk))
    return out.reshape(n, d)


def _alibi_slopes(n):
    ratio = 2.0 ** (-8.0 / n)
    return np.array([ratio ** (i + 1) for i in range(n)], dtype=np.float32)


def _dilated_tables():
    slopes = _alibi_slopes(DIL_HEADS)
    rel = (DIL_BLOCK + np.arange(DIL_BLOCK))[:, None] - np.arange(2 * DIL_BLOCK)[None, :]
    tabs = []
    for window, dilation in DIL_GROUPS:
        valid = (rel >= 0) & (rel <= window // dilation)
        bias = -(slopes[:, None, None] * (rel * dilation).astype(np.float32)[None])
        tabs.append(np.where(valid[None], bias, np.float32(NEG)).astype(np.float32))
    return np.stack(tabs, axis=1)


def _dilated_kernel(q0, q1, q2, k0, k1, k2, kh0, kh1, kh2, v_ref, vh_ref, tab_ref, o_ref,
                    og_ref, lg_ref):
    scale = DIL_HEAD_DIM ** -0.5
    first_window = pl.program_id(2) == 0
    in_prev_block = lax.broadcasted_iota(jnp.int32, (DIL_BLOCK, 2 * DIL_BLOCK), 1) < DIL_BLOCK
    ones = jnp.ones((2 * DIL_BLOCK, LANES), BF16)
    for g, (q_ref, k_ref, kh_ref) in enumerate(((q0, k0, kh0), (q1, k1, kh1), (q2, k2, kh2))):
        dilation = DIL_GROUPS[g][1]
        span = DIL_BLOCK * dilation
        n_blocks = DIL_WINDOW // span
        tab = tab_ref[g]
        tab_first = jnp.where(jnp.logical_and(first_window, in_prev_block), NEG, tab)

        def rows(start):
            if dilation == 1:
                return pl.ds(start, DIL_BLOCK)
            return pl.ds(start, DIL_BLOCK, stride=dilation)

        for r in range(dilation):
            kb = [kh_ref[rows(r), :].astype(BF16)]
            vb = [vh_ref[rows(DIL_WINDOW - span + r), :].astype(BF16)]
            for blk in range(n_blocks):
                kb.append(k_ref[rows(r + blk * span), :].astype(BF16))
                vb.append(v_ref[rows(r + blk * span), :].astype(BF16))
            scores = []
            for blk in range(n_blocks):
                qb = q_ref[rows(r + blk * span), :].astype(BF16)
                keys = jnp.concatenate([kb[blk], kb[blk + 1]], axis=0)
                s = lax.dot_general(qb, keys, _NT, preferred_element_type=F32) * scale
                scores.append(s + (tab_first if blk == 0 else tab))
            for blk in range(n_blocks):
                s = scores[blk]
                m = jnp.max(s, axis=-1, keepdims=True)
                p = jnp.exp(s - m).astype(BF16)
                vals = jnp.concatenate([jnp.concatenate([vb[blk], vb[blk + 1]], axis=0), ones], axis=1)
                res = jnp.dot(p, vals, preferred_element_type=F32)
                den = res[:, LANES:]
                og_ref[g, rows(r + blk * span), :] = res[:, :LANES] / den
                lg_ref[g, rows(r + blk * span), :] = m + jnp.log(den)
    l0, l1, l2 = lg_ref[0], lg_ref[1], lg_ref[2]
    m = jnp.maximum(jnp.maximum(l0, l1), l2)
    e0, e1, e2 = jnp.exp(l0 - m), jnp.exp(l1 - m), jnp.exp(l2 - m)
    mix = e0 * og_ref[0] + e1 * og_ref[1] + e2 * og_ref[2]
    o_ref[...] = (mix / (e0 + e1 + e2)).astype(o_ref.dtype)


def _dilated_attention(proj, batch, seq):
    n = batch * seq
    p4 = proj.reshape(proj.shape[0], batch, seq, LANES)
    n_groups = len(DIL_GROUPS)
    v0 = 2 * n_groups * DIL_HEADS

    def cur(first):
        return pl.BlockSpec((None, None, DIL_WINDOW, LANES), lambda b, hh, w: (first + hh, b, w, 0))

    def halo(first, rows):
        per_window = DIL_WINDOW // rows
        return pl.BlockSpec((None, None, rows, LANES),
                            lambda b, hh, w: (first + hh, b, jnp.maximum(w * per_window - 1, 0), 0))

    q_specs = [cur(g * DIL_HEADS) for g in range(n_groups)]
    k_specs = [cur((n_groups + g) * DIL_HEADS) for g in range(n_groups)]
    kh_specs = [halo((n_groups + g) * DIL_HEADS, DIL_BLOCK * DIL_GROUPS[g][1]) for g in range(n_groups)]
    tabs = _dilated_tables()
    tab_spec = pl.BlockSpec((None,) + tabs.shape[1:], lambda b, hh, w: (hh, 0, 0, 0))
    o = pl.pallas_call(
        _dilated_kernel,
        grid=(batch, DIL_HEADS, seq // DIL_WINDOW),
        in_specs=q_specs + k_specs + kh_specs + [cur(v0), halo(v0, DIL_WINDOW), tab_spec],
        out_specs=pl.BlockSpec((None, None, DIL_WINDOW, LANES), lambda b, hh, w: (hh, b, w, 0)),
        out_shape=jax.ShapeDtypeStruct((DIL_HEADS, batch, seq, LANES), BF16),
        scratch_shapes=[pltpu.VMEM((n_groups, DIL_WINDOW, LANES), F32),
                        pltpu.VMEM((n_groups, DIL_WINDOW, LANES), F32)],
        compiler_params=_params(("parallel", "parallel", "arbitrary")),
        name="dilated_attention",
    )(*([p4] * 11), jnp.asarray(tabs))
    return o.reshape(DIL_HEADS, n, LANES)


def _diff_steps(seq):
    qi, ki = [], []
    for a in range(seq // DIFF_T):
        for b in range(a + 1):
            qi.append(a)
            ki.append(b)
    return np.asarray(qi, np.int32), np.asarray(ki, np.int32)


def _diff_alibi_operands():
    import ml_dtypes
    bf16 = ml_dtypes.bfloat16
    rem = (_alibi_slopes(DIFF_HEADS) * np.float32(LOG2E)).astype(np.float32)
    pieces = []
    for _ in range(3):
        piece = rem.astype(bf16).astype(np.float32)
        pieces.append(piece)
        rem = (rem - piece).astype(np.float32)
    if np.any(rem != 0):
        raise ValueError("slope * log2(e) does not split into three bf16 pieces")
    idx = np.arange(DIFF_T)
    parts = [(idx % 256).astype(np.float32), (idx - idx % 256).astype(np.float32)]
    qa = np.zeros((DIFF_HEADS, DIFF_T, LANES), np.float32)
    kb = np.zeros((DIFF_HEADS, DIFF_T, LANES), np.float32)
    col = 0
    for piece in pieces:
        for part in parts:
            qa[:, :, col] = part[None, :]
            kb[:, :, col] = -piece[:, None]
            qa[:, :, col + 1] = piece[:, None]
            kb[:, :, col + 1] = part[None, :]
            col += 2
    return qa.astype(bf16), kb.astype(bf16)


def _diff_kernel(qi_tab, ki_tab, q_ref, k_ref, v_ref, qa_ref, kb_ref, slope_ref, lq1, lk1, lq2, lk2,
                 sg_ref, o_ref, m1, l1, a1, m2, l2, a2, *, lambda_init):
    step = pl.program_id(2)
    qi = qi_tab[step]
    ki = ki_tab[step]
    slope = slope_ref[...]

    @pl.when(ki == 0)
    def _():
        for m, l, a in ((m1, l1, a1), (m2, l2, a2)):
            m[...] = jnp.full_like(m, NEG)
            l[...] = jnp.zeros_like(l)
            a[...] = jnp.zeros_like(a)

    offset = slope * ((qi - ki) * DIFF_T).astype(F32)

    q = q_ref[...]
    lo = lax.broadcasted_iota(jnp.int32, q.shape, 1) < DIFF_HEAD_DIM
    zero = jnp.zeros_like(q)
    qa = qa_ref[...]
    keys = jnp.concatenate([k_ref[...], kb_ref[...]], axis=1)
    v = v_ref[...]
    ones = jnp.ones((8, DIFF_T), BF16)

    n_strips = DIFF_T // DIFF_STRIP
    units = [(s, c) for c in range(n_strips) for s in range(2)]
    stats = ((m1, l1, a1), (m2, l2, a2))

    def update(on_diagonal):
        queries = [jnp.concatenate([jnp.where(lo, q, zero), qa], axis=1),
                   jnp.concatenate([jnp.where(lo, zero, q), qa], axis=1)]

        def n_keys(c):
            return (c + 1) * DIFF_STRIP if on_diagonal else DIFF_T

        def key_blocks(c):
            return range(0, n_keys(c), DIFF_KEY_BLOCK)

        def score_block(u, r):
            s, c = units[u]
            return lax.dot_general(keys[r:r + DIFF_KEY_BLOCK],
                                   queries[s][c * DIFF_STRIP:(c + 1) * DIFF_STRIP],
                                   _NT, preferred_element_type=F32)

        pending = {u: [score_block(u, r) for r in key_blocks(units[u][1])] for u in (0, 1)}
        for u, (s, c) in enumerate(units):
            m_ref, l_ref, a_ref = stats[s]
            cols = slice(c * DIFF_STRIP, (c + 1) * DIFF_STRIP)
            ts = pending.pop(u)
            if on_diagonal:
                key = lax.broadcasted_iota(jnp.int32, ts[-1].shape, 0)
                qry = lax.broadcasted_iota(jnp.int32, ts[-1].shape, 1)
                ts[-1] = jnp.where(key <= qry, ts[-1], NEG)
            m_old = m_ref[:, cols]
            tile_max = functools.reduce(jnp.maximum, [jnp.max(t, axis=0, keepdims=True) for t in ts])
            m_new = jnp.maximum(m_old, tile_max - offset[:, cols])
            alpha = jnp.exp2(m_old - m_new)
            shift = m_new + offset[:, cols]
            nxt = []
            later = list(key_blocks(units[u + 2][1])) if u + 2 < len(units) else []
            acc = a_ref[:, cols] * alpha
            den = alpha * l_ref[:, cols]
            for b, r in enumerate(key_blocks(c)):
                p = jnp.exp2(ts[b] - shift).astype(BF16)
                acc = acc + lax.dot_general(v[r:r + DIFF_KEY_BLOCK], p, _TN, preferred_element_type=F32)
                den = den + jnp.dot(ones[:, :DIFF_KEY_BLOCK], p, preferred_element_type=F32)[0:1]
                if b < len(later):
                    nxt.append(score_block(u + 2, later[b]))
            nxt.extend(score_block(u + 2, r) for r in later[len(nxt):])
            if later:
                pending[u + 2] = nxt
            a_ref[:, cols] = acc
            l_ref[:, cols] = den
            m_ref[:, cols] = m_new

    pl.when(ki < qi)(lambda: update(False))
    pl.when(ki == qi)(lambda: update(True))

    @pl.when(ki == qi)
    def _():
        lam = (jnp.exp(jnp.sum(lq1[...] * lk1[...], axis=-1, keepdims=True))
               - jnp.exp(jnp.sum(lq2[...] * lk2[...], axis=-1, keepdims=True)) + lambda_init)
        o_t = a1[...] / l1[...] - lam * (a2[...] / l2[...])
        o = _rms(o_t.T) * sg_ref[...] * (1.0 - lambda_init)
        o_ref[...] = o.astype(o_ref.dtype)


def _diff_attention(proj, lq1, lk1, lq2, lk2, subln, lambda_init, batch, seq):
    n = batch * seq
    t = DIFF_T
    p4 = proj.reshape(proj.shape[0], batch, seq, LANES)
    qi_tab, ki_tab = _diff_steps(seq)
    qa, kb = _diff_alibi_operands()
    slopes = np.repeat((_alibi_slopes(DIFF_HEADS) * np.float32(LOG2E))[:, None, None], t, axis=2)
    vec = lambda a: a.reshape(1, -1).astype(F32)
    small = lambda a: pl.BlockSpec(a.shape, lambda b, hh, s, qt, kt: (0, 0))
    lqs = [vec(lq1), vec(lk1), vec(lq2), vec(lk2)]
    sg = vec(subln)
    grid_spec = pltpu.PrefetchScalarGridSpec(
        num_scalar_prefetch=2,
        grid=(batch, DIFF_HEADS, len(qi_tab)),
        in_specs=[
            pl.BlockSpec((None, None, t, LANES), lambda b, hh, s, qt, kt: (hh, b, qt[s], 0)),
            pl.BlockSpec((None, None, t, LANES), lambda b, hh, s, qt, kt: (DIFF_HEADS + hh, b, kt[s], 0)),
            pl.BlockSpec((None, None, t, LANES),
                         lambda b, hh, s, qt, kt: (2 * DIFF_HEADS + hh, b, kt[s], 0)),
            pl.BlockSpec((None, t, LANES), lambda b, hh, s, qt, kt: (hh, 0, 0)),
            pl.BlockSpec((None, t, LANES), lambda b, hh, s, qt, kt: (hh, 0, 0)),
            pl.BlockSpec((None, 1, t), lambda b, hh, s, qt, kt: (hh, 0, 0)),
            small(lqs[0]), small(lqs[1]), small(lqs[2]), small(lqs[3]), small(sg),
        ],
        out_specs=pl.BlockSpec((None, None, t, LANES), lambda b, hh, s, qt, kt: (hh, b, qt[s], 0)),
        scratch_shapes=[
            pltpu.VMEM((1, t), F32), pltpu.VMEM((1, t), F32), pltpu.VMEM((2 * DIFF_HEAD_DIM, t), F32),
            pltpu.VMEM((1, t), F32), pltpu.VMEM((1, t), F32), pltpu.VMEM((2 * DIFF_HEAD_DIM, t), F32),
        ],
    )
    o = pl.pallas_call(
        functools.partial(_diff_kernel, lambda_init=lambda_init),
        grid_spec=grid_spec,
        out_shape=jax.ShapeDtypeStruct((DIFF_HEADS, batch, seq, LANES), BF16),
        compiler_params=_params(("parallel", "parallel", "arbitrary")),
        name="diff_attention",
    )(jnp.asarray(qi_tab), jnp.asarray(ki_tab), p4, p4, p4, jnp.asarray(qa), jnp.asarray(kb),
      jnp.asarray(slopes), *lqs, sg)
    return o.reshape(DIFF_HEADS, n, LANES)


def _slab_out_kernel(o_ref, h_ref, w_ref, out_ref):
    o = jnp.concatenate([o_ref[s] for s in range(o_ref.shape[0])], axis=-1)
    out_ref[...] = h_ref[...] + jnp.dot(o, w_ref[...], preferred_element_type=F32)


def _slab_out(o, h, w_out):
    n, d = h.shape
    tm = OUT_TM
    return pl.pallas_call(
        _slab_out_kernel,
        grid=(n // tm,),
        in_specs=[pl.BlockSpec((o.shape[0], tm, LANES), lambda i: (0, i, 0)),
                  pl.BlockSpec((tm, d), lambda i: (i, 0)),
                  pl.BlockSpec(w_out.shape, lambda i: (0, 0))],
        out_specs=pl.BlockSpec((tm, d), lambda i: (i, 0)),
        out_shape=jax.ShapeDtypeStruct((n, d), F32),
        compiler_params=_params(("parallel",)),
        name="slab_out",
    )(o, h, w_out)


def _ffn_ple_kernel(x_ref, p_ref, g_ref, wi_ref, wo_ref, gg_ref, wg_ref, wp_ref, pg_ref, o_ref):
    x = x_ref[...]
    xn = (_rms(x) * g_ref[...]).astype(BF16)
    h = x
    for c0 in range(0, FFN_HIDDEN, FFN_CHUNK):
        c1 = min(c0 + FFN_CHUNK, FFN_HIDDEN)
        a = jnp.dot(xn, wi_ref[:, c0:c1], preferred_element_type=F32)
        b = jnp.dot(xn, wi_ref[:, FFN_HIDDEN + c0:FFN_HIDDEN + c1], preferred_element_type=F32)
        act = (a * _sigmoid(a) * b).astype(BF16)
        h = h + jnp.dot(act, wo_ref[c0:c1, :], preferred_element_type=F32)
    hn = (_rms(h) * gg_ref[...]).astype(BF16)
    gate = _sigmoid(jnp.dot(hn, wg_ref[...], preferred_element_type=F32))
    e = jnp.dot(p_ref[...].astype(BF16), wp_ref[...], preferred_element_type=F32)
    o_ref[...] = h + gate * (_rms(e) * pg_ref[...])


def _ffn_ple(h, p, ffn_gain, w_in, w_out, gate_gain, w_gate, w_proj, ple_gain):
    n, d = h.shape
    tm = FFN_TM
    row = lambda i: (i, 0)
    fixed = lambda i: (0, 0)
    resident = lambda a: pl.BlockSpec(a.shape, fixed, pipeline_mode=pl.Buffered(1))
    return pl.pallas_call(
        _ffn_ple_kernel,
        grid=(n // tm,),
        in_specs=[pl.BlockSpec((tm, d), row), pl.BlockSpec((tm, p.shape[1]), row),
                  pl.BlockSpec((1, d), fixed), resident(w_in), resident(w_out),
                  pl.BlockSpec((1, d), fixed), resident(w_gate), resident(w_proj),
                  pl.BlockSpec((1, d), fixed)],
        out_specs=pl.BlockSpec((tm, d), row),
        out_shape=jax.ShapeDtypeStruct((n, d), F32),
        compiler_params=_params(("parallel",)),
        name="ffn_ple",
    )(h, p, ffn_gain.reshape(1, d), w_in, w_out, gate_gain.reshape(1, d), w_gate, w_proj,
      ple_gain.reshape(1, d))


def _diff_lambda_init(layer_idx):
    return 0.8 - 0.6 * math.exp(-0.3 * layer_idx)


def _tile_gains(rows):
    return jnp.stack([jnp.tile(r, LANES // r.shape[0]) for r in rows])[:, None, :].astype(F32)


def kernel(x, p, mix_norm, ffn_norm, a_w_in, a_w_out, b_w_in, b_q_norm, b_k_norm, b_w_out,
           c_w_in, c_q_norm, c_k_norm, c_lambda_q1, c_lambda_k1, c_lambda_q2, c_lambda_k2,
           c_subln, c_w_out, ffn_w_in, ffn_w_out, ple_w_proj, ple_norm, ple_gate_norm, ple_w_gate):
    batch, seq, d = x.shape
    depth = p.shape[0]
    n = batch * seq
    h = x.reshape(n, d)
    bf = lambda w: w.astype(BF16)
    for i in range(depth):
        kind, j = i % N_MIXERS, i // N_MIXERS
        if kind == 0:
            proj = _norm_proj(h, mix_norm[i], bf(a_w_in[j]))
            h = _retention(proj, h, bf(a_w_out[j]), batch, seq)
        elif kind == 1:
            ones = jnp.ones((LANES,), F32)
            gains = _tile_gains([b_q_norm[j]] * 3 + [b_k_norm[j]] * 3 + [ones])
            proj = _norm_proj(h, mix_norm[i], bf(b_w_in[j]), gains, "full", 6, out_dtype=F32)
            h = _slab_out(_dilated_attention(proj, batch, seq), h, bf(b_w_out[j]))
        else:
            ones = jnp.ones((LANES,), F32)
            gains = _tile_gains([c_q_norm[j] * (DIFF_HEAD_DIM ** -0.5 * LOG2E), c_k_norm[j], ones])
            proj = _norm_proj(h, mix_norm[i], bf(c_w_in[j]), gains, "half", 2)
            o = _diff_attention(proj, c_lambda_q1[j], c_lambda_k1[j], c_lambda_q2[j],
                                c_lambda_k2[j], c_subln[j], _diff_lambda_init(i), batch, seq)
            h = _slab_out(o, h, bf(c_w_out[j]))
        h = _ffn_ple(h, p[i].reshape(n, PLE_DIM), ffn_norm[i], bf(ffn_w_in[i]), bf(ffn_w_out[i]),
                     ple_gate_norm[i], bf(ple_w_gate[i]), bf(ple_w_proj[i]), ple_norm[i])
    return h.reshape(batch, seq, d)
```

```python
import functools
import math

import numpy as np
import jax
import jax.numpy as jnp
from jax import lax
from jax.experimental import pallas as pl
from jax.experimental.pallas import tpu as pltpu

D_MODEL = 1024
PLE_DIM = 256
N_MIXERS = 3
RMS_EPS = 1e-6
NEG = -1e30

RET_HEADS = 4
RET_DK = 256
RET_DV = 512
RET_CHUNK = 128
RET_IN = 2 * RET_HEADS * RET_DK + 2 * RET_HEADS * RET_DV

DIL_GROUPS = ((128, 1), (512, 4), (2048, 16))
DIL_HEADS = 8
DIL_HEAD_DIM = 128
DIL_BLOCK = 128
DIL_QK = len(DIL_GROUPS) * DIL_HEADS * DIL_HEAD_DIM
DIL_IN = 2 * DIL_QK + DIL_HEADS * DIL_HEAD_DIM

DIFF_HEADS = 8
DIFF_HEAD_DIM = 64
DIFF_QK = 2 * DIFF_HEADS * DIFF_HEAD_DIM
DIFF_V = DIFF_HEADS * 2 * DIFF_HEAD_DIM
DIFF_IN = 2 * DIFF_QK + DIFF_V

FFN_HIDDEN = 2816

LANES = 128
V7X_VMEM_BYTES = 64 * 1024 * 1024
VMEM_LIMIT_BYTES = V7X_VMEM_BYTES - 8 * 1024 * 1024

PROJ_TM = 1024
PROJ_TN = 1024
FFN_TM = 512
FFN_CHUNK = 1024
RET_BLOCK = 256
RET_BATCH = 2
DIL_WINDOW = 2048
DIFF_T = 2048
DIFF_STRIP = 256
DIFF_KEY_BLOCK = 256
OUT_TM = 512

LOG2E = math.log2(math.e)

F32 = jnp.float32
BF16 = jnp.bfloat16

_NT = (((1,), (1,)), ((), ()))
_TN = (((0,), (0,)), ((), ()))


def _params(semantics):
    return pltpu.CompilerParams(dimension_semantics=semantics, vmem_limit_bytes=VMEM_LIMIT_BYTES)


def _rms(x):
    return x * lax.rsqrt(jnp.mean(x * x, axis=-1, keepdims=True) + RMS_EPS)


def _sigmoid(x):
    return 1.0 / (1.0 + jnp.exp(-x))


def _norm_proj_kernel(x_ref, g_ref, w_ref, hg_ref, o_ref, xn_ref, *, head_norm, n_norm_tiles):
    j = pl.program_id(1)

    @pl.when(j == 0)
    def _():
        xn_ref[...] = (_rms(x_ref[...]) * g_ref[...]).astype(BF16)

    y = jnp.dot(xn_ref[...], w_ref[...], preferred_element_type=F32)
    slabs = y.shape[1] // LANES

    def store(fn):
        for s in range(slabs):
            o_ref[s] = fn(y[:, s * LANES:(s + 1) * LANES]).astype(o_ref.dtype)

    def full_norm(z):
        return _rms(z) * hg_ref[...]

    def half_norm(z):
        lo = lax.broadcasted_iota(jnp.int32, z.shape, 1) < (LANES // 2)
        zz = z * z
        ss_lo = jnp.sum(jnp.where(lo, zz, 0.0), axis=-1, keepdims=True)
        ss_hi = jnp.sum(jnp.where(lo, 0.0, zz), axis=-1, keepdims=True)
        inv = jnp.where(lo, lax.rsqrt(ss_lo / (LANES // 2) + RMS_EPS),
                        lax.rsqrt(ss_hi / (LANES // 2) + RMS_EPS))
        return z * inv * hg_ref[...]

    if head_norm is None:
        store(lambda z: z)
    else:
        fn = full_norm if head_norm == "full" else half_norm
        pl.when(j < n_norm_tiles)(lambda: store(fn))
        pl.when(j >= n_norm_tiles)(lambda: store(lambda z: z))


def _norm_proj(h, gain, w, head_gains=None, head_norm=None, n_norm_tiles=0, out_dtype=BF16):
    n, d = h.shape
    n_out = w.shape[1]
    n_tiles = n_out // PROJ_TN
    slabs = PROJ_TN // LANES
    if head_gains is None:
        head_gains = jnp.ones((n_tiles, 1, LANES), F32)
    kern = functools.partial(_norm_proj_kernel, head_norm=head_norm, n_norm_tiles=n_norm_tiles)
    return pl.pallas_call(
        kern,
        grid=(n // PROJ_TM, n_tiles),
        in_specs=[
            pl.BlockSpec((PROJ_TM, d), lambda i, j: (i, 0)),
            pl.BlockSpec((1, d), lambda i, j: (0, 0)),
            pl.BlockSpec((d, PROJ_TN), lambda i, j: (0, j)),
            pl.BlockSpec((None, 1, LANES), lambda i, j: (j, 0, 0)),
        ],
        out_specs=pl.BlockSpec((slabs, PROJ_TM, LANES), lambda i, j: (j, i, 0)),
        out_shape=jax.ShapeDtypeStruct((n_out // LANES, n, LANES), out_dtype),
        scratch_shapes=[pltpu.VMEM((PROJ_TM, d), BF16)],
        compiler_params=_params(("parallel", "arbitrary")),
        name="norm_proj",
    )(h, gain.reshape(1, d), w, head_gains)


def _retention_tables():
    h = np.arange(RET_HEADS, dtype=np.float32)
    log_g = np.log(np.float32(1.0) - np.float32(2.0) ** (np.float32(-5.0) - h)).astype(np.float32)
    pos = np.arange(RET_BLOCK, dtype=np.float32)
    rel = pos[:, None] - pos[None, :]
    scale = np.float32(RET_DK ** -0.5)
    din = np.where(rel >= 0, np.exp(np.maximum(rel, 0.0)[None] * log_g[:, None, None]), 0.0)
    dq = np.exp((pos + 1.0)[None] * log_g[:, None])
    dk = np.exp((RET_BLOCK - 1.0 - pos)[None] * log_g[:, None])
    dchunk = np.exp(RET_BLOCK * log_g)
    return ((din * scale).astype(np.float32), dq.astype(np.float32)[:, :, None],
            (dk * scale).astype(np.float32)[:, :, None], tuple(float(v) for v in dchunk))


def _retention_kernel(q_ref, k_ref, v_ref, g_ref, h_ref, wo_ref, din_ref, dq_ref, dk_ref,
                      o_ref, r_ref, y_ref, *, dchunk):
    @pl.when(pl.program_id(1) == 0)
    def _():
        r_ref[...] = jnp.zeros_like(r_ref)

    qs, vs = RET_DK // LANES, RET_DV // LANES
    chains = [(bb, hh) for bb in range(RET_BATCH) for hh in range(RET_HEADS)]

    def cat(ref, bb, first, count):
        return jnp.concatenate([ref[first + s, bb] for s in range(count)], axis=-1)

    qc = [cat(q_ref, bb, hh * qs, qs) for bb, hh in chains]
    kc = [cat(k_ref, bb, hh * qs, qs) for bb, hh in chains]
    vc = [cat(v_ref, bb, hh * vs, vs) for bb, hh in chains]
    att = [(lax.dot_general(qc[i], kc[i], _NT, preferred_element_type=F32) * din_ref[hh]).astype(BF16)
           for i, (bb, hh) in enumerate(chains)]
    state = [r_ref[bb, hh] for bb, hh in chains]
    y = [jnp.dot(att[i], vc[i], preferred_element_type=F32)
         + jnp.dot(qc[i], state[i].astype(BF16), preferred_element_type=F32) * dq_ref[hh]
         for i, (bb, hh) in enumerate(chains)]
    for i, (bb, hh) in enumerate(chains):
        kd = (kc[i].astype(F32) * dk_ref[hh]).astype(BF16)
        r_ref[bb, hh] = state[i] * dchunk[hh] + lax.dot_general(kd, vc[i], _TN,
                                                                preferred_element_type=F32)
    for i, (bb, hh) in enumerate(chains):
        gc = cat(g_ref, bb, hh * vs, vs).astype(F32)
        y_ref[bb * RET_BLOCK:(bb + 1) * RET_BLOCK, hh * RET_DV:(hh + 1) * RET_DV] = (
            gc * _sigmoid(gc) * _rms(y[i])).astype(BF16)
    out = jnp.dot(y_ref[...], wo_ref[...], preferred_element_type=F32)
    for bb in range(RET_BATCH):
        o_ref[bb] = h_ref[bb] + out[bb * RET_BLOCK:(bb + 1) * RET_BLOCK]


def _retention(proj, h, w_out, batch, seq):
    n, d = h.shape
    din, dq, dk, dchunk = _retention_tables()
    p4 = proj.reshape(proj.shape[0], batch, seq, LANES)
    h3 = h.reshape(batch, seq, d)
    nq = RET_HEADS * RET_DK // LANES
    nv = RET_HEADS * RET_DV // LANES
    t, nb = RET_BLOCK, RET_BATCH
    slab_spec = lambda cnt, blk: pl.BlockSpec((cnt, nb, t, LANES), lambda b, i: (blk, b, i, 0))
    const3 = lambda a: pl.BlockSpec(a.shape, lambda b, i: (0, 0, 0))
    out = pl.pallas_call(
        functools.partial(_retention_kernel, dchunk=dchunk),
        grid=(batch // nb, seq // t),
        in_specs=[
            slab_spec(nq, 0), slab_spec(nq, 1), slab_spec(nv, 1), slab_spec(nv, 2),
            pl.BlockSpec((nb, t, d), lambda b, i: (b, i, 0)),
            pl.BlockSpec(w_out.shape, lambda b, i: (0, 0)),
            const3(din), const3(dq), const3(dk),
        ],
        out_specs=pl.BlockSpec((nb, t, d), lambda b, i: (b, i, 0)),
        out_shape=jax.ShapeDtypeStruct((batch, seq, d), F32),
        scratch_shapes=[pltpu.VMEM((nb, RET_HEADS, RET_DK, RET_DV), F32),
                        pltpu.VMEM((nb * t, RET_HEADS * RET_DV), BF16)],
        compiler_params=_params(("parallel", "arbitrary")),
        name="retention",
    )(p4, p4, p4, p4, h3, w_out, jnp.asarray(din), jnp.asarray(dq), jnp.asarray(dk))
    return out.reshape(n, d)


def _alibi_slopes(n):
    ratio = 2.0 ** (-8.0 / n)
    return np.array([ratio ** (i + 1) for i in range(n)], dtype=np.float32)


def _dilated_tables():
    slopes = _alibi_slopes(DIL_HEADS)
    rel = (DIL_BLOCK + np.arange(DIL_BLOCK))[:, None] - np.arange(2 * DIL_BLOCK)[None, :]
    tabs = []
    for window, dilation in DIL_GROUPS:
        valid = (rel >= 0) & (rel <= window // dilation)
        bias = -(slopes[:, None, None] * (rel * dilation).astype(np.float32)[None]) * np.float32(LOG2E)
        tabs.append(np.where(valid[None], bias, np.float32(NEG)).astype(np.float32))
    return np.stack(tabs, axis=1)


def _dilated_kernel(q0, q1, q2, k0, k1, k2, kh0, kh1, kh2, v_ref, vh_ref, tab_ref, o_ref,
                    og_ref, lg_ref):
    first_window = pl.program_id(2) == 0
    in_prev_block = lax.broadcasted_iota(jnp.int32, (DIL_BLOCK, 2 * DIL_BLOCK), 1) < DIL_BLOCK
    ones = jnp.ones((2 * DIL_BLOCK, LANES), BF16)
    for g, (q_ref, k_ref, kh_ref) in enumerate(((q0, k0, kh0), (q1, k1, kh1), (q2, k2, kh2))):
        dilation = DIL_GROUPS[g][1]
        span = DIL_BLOCK * dilation
        n_blocks = DIL_WINDOW // span
        tab = tab_ref[g]
        tab_first = jnp.where(jnp.logical_and(first_window, in_prev_block), NEG, tab)

        def rows(start):
            if dilation == 1:
                return pl.ds(start, DIL_BLOCK)
            return pl.ds(start, DIL_BLOCK, stride=dilation)

        for r in range(dilation):
            kb = [kh_ref[rows(r), :].astype(BF16)]
            vb = [vh_ref[rows(DIL_WINDOW - span + r), :].astype(BF16)]
            for blk in range(n_blocks):
                kb.append(k_ref[rows(r + blk * span), :].astype(BF16))
                vb.append(v_ref[rows(r + blk * span), :].astype(BF16))
            scores = []
            for blk in range(n_blocks):
                qb = q_ref[rows(r + blk * span), :].astype(BF16)
                keys = jnp.concatenate([kb[blk], kb[blk + 1]], axis=0)
                s = lax.dot_general(qb, keys, _NT, preferred_element_type=F32)
                scores.append(s + (tab_first if blk == 0 else tab))
            for blk in range(n_blocks):
                s = scores[blk]
                m = jnp.max(s, axis=-1, keepdims=True)
                p = jnp.exp2(s - m).astype(BF16)
                vals = jnp.concatenate([jnp.concatenate([vb[blk], vb[blk + 1]], axis=0), ones], axis=1)
                res = jnp.dot(p, vals, preferred_element_type=F32)
                den = res[:, LANES:]
                og_ref[g, rows(r + blk * span), :] = res[:, :LANES] / den
                lg_ref[g, rows(r + blk * span), :] = m + jnp.log2(den)
    l0, l1, l2 = lg_ref[0], lg_ref[1], lg_ref[2]
    m = jnp.maximum(jnp.maximum(l0, l1), l2)
    e0, e1, e2 = jnp.exp2(l0 - m), jnp.exp2(l1 - m), jnp.exp2(l2 - m)
    mix = e0 * og_ref[0] + e1 * og_ref[1] + e2 * og_ref[2]
    o_ref[...] = (mix / (e0 + e1 + e2)).astype(o_ref.dtype)


def _dilated_attention(proj, batch, seq):
    n = batch * seq
    p4 = proj.reshape(proj.shape[0], batch, seq, LANES)
    n_groups = len(DIL_GROUPS)
    v0 = 2 * n_groups * DIL_HEADS

    def cur(first):
        return pl.BlockSpec((None, None, DIL_WINDOW, LANES), lambda b, hh, w: (first + hh, b, w, 0))

    def halo(first, rows):
        per_window = DIL_WINDOW // rows
        return pl.BlockSpec((None, None, rows, LANES),
                            lambda b, hh, w: (first + hh, b, jnp.maximum(w * per_window - 1, 0), 0))

    q_specs = [cur(g * DIL_HEADS) for g in range(n_groups)]
    k_specs = [cur((n_groups + g) * DIL_HEADS) for g in range(n_groups)]
    kh_specs = [halo((n_groups + g) * DIL_HEADS, DIL_BLOCK * DIL_GROUPS[g][1]) for g in range(n_groups)]
    tabs = _dilated_tables()
    tab_spec = pl.BlockSpec((None,) + tabs.shape[1:], lambda b, hh, w: (hh, 0, 0, 0))
    o = pl.pallas_call(
        _dilated_kernel,
        grid=(batch, DIL_HEADS, seq // DIL_WINDOW),
        in_specs=q_specs + k_specs + kh_specs + [cur(v0), halo(v0, DIL_WINDOW), tab_spec],
        out_specs=pl.BlockSpec((None, None, DIL_WINDOW, LANES), lambda b, hh, w: (hh, b, w, 0)),
        out_shape=jax.ShapeDtypeStruct((DIL_HEADS, batch, seq, LANES), BF16),
        scratch_shapes=[pltpu.VMEM((n_groups, DIL_WINDOW, LANES), F32),
                        pltpu.VMEM((n_groups, DIL_WINDOW, LANES), F32)],
        compiler_params=_params(("parallel", "parallel", "arbitrary")),
        name="dilated_attention",
    )(*([p4] * 11), jnp.asarray(tabs))
    return o.reshape(DIL_HEADS, n, LANES)


def _diff_steps(seq):
    qi, ki = [], []
    for a in range(seq // DIFF_T):
        for b in range(a + 1):
            qi.append(a)
            ki.append(b)
    return np.asarray(qi, np.int32), np.asarray(ki, np.int32)


def _diff_alibi_operands():
    import ml_dtypes
    bf16 = ml_dtypes.bfloat16
    rem = (_alibi_slopes(DIFF_HEADS) * np.float32(LOG2E)).astype(np.float32)
    pieces = []
    for _ in range(3):
        piece = rem.astype(bf16).astype(np.float32)
        pieces.append(piece)
        rem = (rem - piece).astype(np.float32)
    if np.any(rem != 0):
        raise ValueError("slope * log2(e) does not split into three bf16 pieces")
    idx = np.arange(DIFF_T)
    parts = [(idx % 256).astype(np.float32), (idx - idx % 256).astype(np.float32)]
    qa = np.zeros((DIFF_HEADS, DIFF_T, LANES), np.float32)
    kb = np.zeros((DIFF_HEADS, DIFF_T, LANES), np.float32)
    col = 0
    for piece in pieces:
        for part in parts:
            qa[:, :, col] = part[None, :]
            kb[:, :, col] = -piece[:, None]
            qa[:, :, col + 1] = piece[:, None]
            kb[:, :, col + 1] = part[None, :]
            col += 2
    return qa.astype(bf16), kb.astype(bf16)


def _diff_kernel(qi_tab, ki_tab, q_ref, k_ref, v_ref, qa_ref, kb_ref, slope_ref, lq1, lk1, lq2, lk2,
                 sg_ref, o_ref, m1, l1, a1, m2, l2, a2, *, lambda_init):
    step = pl.program_id(2)
    qi = qi_tab[step]
    ki = ki_tab[step]
    slope = slope_ref[...]

    @pl.when(ki == 0)
    def _():
        for m, l, a in ((m1, l1, a1), (m2, l2, a2)):
            m[...] = jnp.full_like(m, NEG)
            l[...] = jnp.zeros_like(l)
            a[...] = jnp.zeros_like(a)

    offset = slope * ((qi - ki) * DIFF_T).astype(F32)

    q = q_ref[...]
    lo = lax.broadcasted_iota(jnp.int32, q.shape, 1) < DIFF_HEAD_DIM
    zero = jnp.zeros_like(q)
    qa = qa_ref[...]
    keys = jnp.concatenate([k_ref[...], kb_ref[...]], axis=1)
    v = v_ref[...]
    ones = jnp.ones((8, DIFF_T), BF16)

    n_strips = DIFF_T // DIFF_STRIP
    units = [(s, c) for c in range(n_strips) for s in range(2)]
    stats = ((m1, l1, a1), (m2, l2, a2))

    def update(on_diagonal):
        queries = [jnp.concatenate([jnp.where(lo, q, zero), qa], axis=1),
                   jnp.concatenate([jnp.where(lo, zero, q), qa], axis=1)]

        def n_keys(c):
            return (c + 1) * DIFF_STRIP if on_diagonal else DIFF_T

        def key_blocks(c):
            return range(0, n_keys(c), DIFF_KEY_BLOCK)

        def score_block(u, r):
            s, c = units[u]
            return lax.dot_general(keys[r:r + DIFF_KEY_BLOCK],
                                   queries[s][c * DIFF_STRIP:(c + 1) * DIFF_STRIP],
                                   _NT, preferred_element_type=F32)

        pending = {u: [score_block(u, r) for r in key_blocks(units[u][1])] for u in (0, 1)}
        for u, (s, c) in enumerate(units):
            m_ref, l_ref, a_ref = stats[s]
            cols = slice(c * DIFF_STRIP, (c + 1) * DIFF_STRIP)
            ts = pending.pop(u)
            if on_diagonal:
                key = lax.broadcasted_iota(jnp.int32, ts[-1].shape, 0)
                qry = lax.broadcasted_iota(jnp.int32, ts[-1].shape, 1)
                ts[-1] = jnp.where(key <= qry, ts[-1], NEG)
            m_old = m_ref[:, cols]
            tile_max = functools.reduce(jnp.maximum, [jnp.max(t, axis=0, keepdims=True) for t in ts])
            m_new = jnp.maximum(m_old, tile_max - offset[:, cols])
            alpha = jnp.exp2(m_old - m_new)
            shift = m_new + offset[:, cols]
            nxt = []
            later = list(key_blocks(units[u + 2][1])) if u + 2 < len(units) else []
            acc = a_ref[:, cols] * alpha
            den = alpha * l_ref[:, cols]
            for b, r in enumerate(key_blocks(c)):
                p = jnp.exp2(ts[b] - shift).astype(BF16)
                acc = acc + lax.dot_general(v[r:r + DIFF_KEY_BLOCK], p, _TN, preferred_element_type=F32)
                den = den + jnp.dot(ones[:, :DIFF_KEY_BLOCK], p, preferred_element_type=F32)[0:1]
                if b < len(later):
                    nxt.append(score_block(u + 2, later[b]))
            nxt.extend(score_block(u + 2, r) for r in later[len(nxt):])
            if later:
                pending[u + 2] = nxt
            a_ref[:, cols] = acc
            l_ref[:, cols] = den
            m_ref[:, cols] = m_new

    pl.when(ki < qi)(lambda: update(False))
    pl.when(ki == qi)(lambda: update(True))

    @pl.when(ki == qi)
    def _():
        lam = (jnp.exp(jnp.sum(lq1[...] * lk1[...], axis=-1, keepdims=True))
               - jnp.exp(jnp.sum(lq2[...] * lk2[...], axis=-1, keepdims=True)) + lambda_init)
        o_t = a1[...] / l1[...] - lam * (a2[...] / l2[...])
        o = _rms(o_t.T) * sg_ref[...] * (1.0 - lambda_init)
        o_ref[...] = o.astype(o_ref.dtype)


def _diff_attention(proj, lq1, lk1, lq2, lk2, subln, lambda_init, batch, seq):
    n = batch * seq
    t = DIFF_T
    p4 = proj.reshape(proj.shape[0], batch, seq, LANES)
    qi_tab, ki_tab = _diff_steps(seq)
    qa, kb = _diff_alibi_operands()
    slopes = np.repeat((_alibi_slopes(DIFF_HEADS) * np.float32(LOG2E))[:, None, None], t, axis=2)
    vec = lambda a: a.reshape(1, -1).astype(F32)
    small = lambda a: pl.BlockSpec(a.shape, lambda b, hh, s, qt, kt: (0, 0))
    lqs = [vec(lq1), vec(lk1), vec(lq2), vec(lk2)]
    sg = vec(subln)
    grid_spec = pltpu.PrefetchScalarGridSpec(
        num_scalar_prefetch=2,
        grid=(batch, DIFF_HEADS, len(qi_tab)),
        in_specs=[
            pl.BlockSpec((None, None, t, LANES), lambda b, hh, s, qt, kt: (hh, b, qt[s], 0)),
            pl.BlockSpec((None, None, t, LANES), lambda b, hh, s, qt, kt: (DIFF_HEADS + hh, b, kt[s], 0)),
            pl.BlockSpec((None, None, t, LANES),
                         lambda b, hh, s, qt, kt: (2 * DIFF_HEADS + hh, b, kt[s], 0)),
            pl.BlockSpec((None, t, LANES), lambda b, hh, s, qt, kt: (hh, 0, 0)),
            pl.BlockSpec((None, t, LANES), lambda b, hh, s, qt, kt: (hh, 0, 0)),
            pl.BlockSpec((None, 1, t), lambda b, hh, s, qt, kt: (hh, 0, 0)),
            small(lqs[0]), small(lqs[1]), small(lqs[2]), small(lqs[3]), small(sg),
        ],
        out_specs=pl.BlockSpec((None, None, t, LANES), lambda b, hh, s, qt, kt: (hh, b, qt[s], 0)),
        scratch_shapes=[
            pltpu.VMEM((1, t), F32), pltpu.VMEM((1, t), F32), pltpu.VMEM((2 * DIFF_HEAD_DIM, t), F32),
            pltpu.VMEM((1, t), F32), pltpu.VMEM((1, t), F32), pltpu.VMEM((2 * DIFF_HEAD_DIM, t), F32),
        ],
    )
    o = pl.pallas_call(
        functools.partial(_diff_kernel, lambda_init=lambda_init),
        grid_spec=grid_spec,
        out_shape=jax.ShapeDtypeStruct((DIFF_HEADS, batch, seq, LANES), BF16),
        compiler_params=_params(("parallel", "parallel", "arbitrary")),
        name="diff_attention",
    )(jnp.asarray(qi_tab), jnp.asarray(ki_tab), p4, p4, p4, jnp.asarray(qa), jnp.asarray(kb),
      jnp.asarray(slopes), *lqs, sg)
    return o.reshape(DIFF_HEADS, n, LANES)


def _slab_out_kernel(o_ref, h_ref, w_ref, out_ref):
    o = jnp.concatenate([o_ref[s] for s in range(o_ref.shape[0])], axis=-1)
    out_ref[...] = h_ref[...] + jnp.dot(o, w_ref[...], preferred_element_type=F32)


def _slab_out(o, h, w_out):
    n, d = h.shape
    tm = OUT_TM
    return pl.pallas_call(
        _slab_out_kernel,
        grid=(n // tm,),
        in_specs=[pl.BlockSpec((o.shape[0], tm, LANES), lambda i: (0, i, 0)),
                  pl.BlockSpec((tm, d), lambda i: (i, 0)),
                  pl.BlockSpec(w_out.shape, lambda i: (0, 0))],
        out_specs=pl.BlockSpec((tm, d), lambda i: (i, 0)),
        out_shape=jax.ShapeDtypeStruct((n, d), F32),
        compiler_params=_params(("parallel",)),
        name="slab_out",
    )(o, h, w_out)


def _ffn_ple_kernel(x_ref, p_ref, g_ref, wi_ref, wo_ref, gg_ref, wg_ref, wp_ref, pg_ref, o_ref):
    x = x_ref[...]
    xn = (_rms(x) * g_ref[...]).astype(BF16)
    h = x
    for c0 in range(0, FFN_HIDDEN, FFN_CHUNK):
        c1 = min(c0 + FFN_CHUNK, FFN_HIDDEN)
        a = jnp.dot(xn, wi_ref[:, c0:c1], preferred_element_type=F32)
        b = jnp.dot(xn, wi_ref[:, FFN_HIDDEN + c0:FFN_HIDDEN + c1], preferred_element_type=F32)
        act = (a * _sigmoid(a) * b).astype(BF16)
        h = h + jnp.dot(act, wo_ref[c0:c1, :], preferred_element_type=F32)
    hn = (_rms(h) * gg_ref[...]).astype(BF16)
    gate = _sigmoid(jnp.dot(hn, wg_ref[...], preferred_element_type=F32))
    e = jnp.dot(p_ref[...].astype(BF16), wp_ref[...], preferred_element_type=F32)
    o_ref[...] = h + gate * (_rms(e) * pg_ref[...])


def _ffn_ple(h, p, ffn_gain, w_in, w_out, gate_gain, w_gate, w_proj, ple_gain):
    n, d = h.shape
    tm = FFN_TM
    row = lambda i: (i, 0)
    fixed = lambda i: (0, 0)
    resident = lambda a: pl.BlockSpec(a.shape, fixed, pipeline_mode=pl.Buffered(1))
    return pl.pallas_call(
        _ffn_ple_kernel,
        grid=(n // tm,),
        in_specs=[pl.BlockSpec((tm, d), row), pl.BlockSpec((tm, p.shape[1]), row),
                  pl.BlockSpec((1, d), fixed), resident(w_in), resident(w_out),
                  pl.BlockSpec((1, d), fixed), resident(w_gate), resident(w_proj),
                  pl.BlockSpec((1, d), fixed)],
        out_specs=pl.BlockSpec((tm, d), row),
        out_shape=jax.ShapeDtypeStruct((n, d), F32),
        compiler_params=_params(("parallel",)),
        name="ffn_ple",
    )(h, p, ffn_gain.reshape(1, d), w_in, w_out, gate_gain.reshape(1, d), w_gate, w_proj,
      ple_gain.reshape(1, d))


def _diff_lambda_init(layer_idx):
    return 0.8 - 0.6 * math.exp(-0.3 * layer_idx)


def _tile_gains(rows):
    return jnp.stack([jnp.tile(r, LANES // r.shape[0]) for r in rows])[:, None, :].astype(F32)


def kernel(x, p, mix_norm, ffn_norm, a_w_in, a_w_out, b_w_in, b_q_norm, b_k_norm, b_w_out,
           c_w_in, c_q_norm, c_k_norm, c_lambda_q1, c_lambda_k1, c_lambda_q2, c_lambda_k2,
           c_subln, c_w_out, ffn_w_in, ffn_w_out, ple_w_proj, ple_norm, ple_gate_norm, ple_w_gate):
    batch, seq, d = x.shape
    depth = p.shape[0]
    n = batch * seq
    h = x.reshape(n, d)
    bf = lambda w: w.astype(BF16)
    for i in range(depth):
        kind, j = i % N_MIXERS, i // N_MIXERS
        if kind == 0:
            proj = _norm_proj(h, mix_norm[i], bf(a_w_in[j]))
            h = _retention(proj, h, bf(a_w_out[j]), batch, seq)
        elif kind == 1:
            ones = jnp.ones((LANES,), F32)
            q_gain = b_q_norm[j] * (DIL_HEAD_DIM ** -0.5 * LOG2E)
            gains = _tile_gains([q_gain] * 3 + [b_k_norm[j]] * 3 + [ones])
            proj = _norm_proj(h, mix_norm[i], bf(b_w_in[j]), gains, "full", 6, out_dtype=F32)
            h = _slab_out(_dilated_attention(proj, batch, seq), h, bf(b_w_out[j]))
        else:
            ones = jnp.ones((LANES,), F32)
            gains = _tile_gains([c_q_norm[j] * (DIFF_HEAD_DIM ** -0.5 * LOG2E), c_k_norm[j], ones])
            proj = _norm_proj(h, mix_norm[i], bf(c_w_in[j]), gains, "half", 2)
            o = _diff_attention(proj, c_lambda_q1[j], c_lambda_k1[j], c_lambda_q2[j],
                                c_lambda_k2[j], c_subln[j], _diff_lambda_init(i), batch, seq)
            h = _slab_out(o, h, bf(c_w_out[j]))
        h = _ffn_ple(h, p[i].reshape(n, PLE_DIM), ffn_norm[i], bf(ffn_w_in[i]), bf(ffn_w_out[i]),
                     ple_gate_norm[i], bf(ple_w_gate[i]), bf(ple_w_proj[i]), ple_norm[i])
    return h.reshape(batch, seq, d)
```

```python
import functools
import math

import numpy as np
import jax
import jax.numpy as jnp
from jax import lax
from jax.experimental import pallas as pl
from jax.experimental.pallas import tpu as pltpu

D_MODEL = 1024
PLE_DIM = 256
N_MIXERS = 3
RMS_EPS = 1e-6
NEG = -1e30

RET_HEADS = 4
RET_DK = 256
RET_DV = 512
RET_CHUNK = 128
RET_IN = 2 * RET_HEADS * RET_DK + 2 * RET_HEADS * RET_DV

DIL_GROUPS = ((128, 1), (512, 4), (2048, 16))
DIL_HEADS = 8
DIL_HEAD_DIM = 128
DIL_BLOCK = 128
DIL_QK = len(DIL_GROUPS) * DIL_HEADS * DIL_HEAD_DIM
DIL_IN = 2 * DIL_QK + DIL_HEADS * DIL_HEAD_DIM

DIFF_HEADS = 8
DIFF_HEAD_DIM = 64
DIFF_QK = 2 * DIFF_HEADS * DIFF_HEAD_DIM
DIFF_V = DIFF_HEADS * 2 * DIFF_HEAD_DIM
DIFF_IN = 2 * DIFF_QK + DIFF_V

FFN_HIDDEN = 2816

LANES = 128
V7X_VMEM_BYTES = 64 * 1024 * 1024
VMEM_LIMIT_BYTES = V7X_VMEM_BYTES - 8 * 1024 * 1024

PROJ_TM = 1024
PROJ_TN = 1024
FFN_TM = 512
FFN_CHUNK = 1024
RET_BLOCK = 256
RET_BATCH = 2
DIL_WINDOW = 2048
DIFF_T = 2048
DIFF_STRIP = 256

LOG2E = math.log2(math.e)

F32 = jnp.float32
BF16 = jnp.bfloat16

_NT = (((1,), (1,)), ((), ()))
_TN = (((0,), (0,)), ((), ()))


def _params(semantics):
    return pltpu.CompilerParams(dimension_semantics=semantics, vmem_limit_bytes=VMEM_LIMIT_BYTES)


def _layer_spec(layer, block, index_map, **kwargs):
    return pl.BlockSpec((None,) + tuple(block), lambda *g: (layer,) + tuple(index_map(*g)), **kwargs)


def _rms(x):
    return x * lax.rsqrt(jnp.mean(x * x, axis=-1, keepdims=True) + RMS_EPS)


def _sigmoid(x):
    return 1.0 / (1.0 + jnp.exp(-x))


def _norm_proj_kernel(x_ref, g_ref, w_ref, hg_ref, o_ref, xn_ref, *, head_norm, n_norm_tiles):
    j = pl.program_id(1)

    @pl.when(j == 0)
    def _():
        xn_ref[...] = (_rms(x_ref[...]) * g_ref[...]).astype(BF16)

    y = jnp.dot(xn_ref[...], w_ref[...], preferred_element_type=F32)
    slabs = y.shape[1] // LANES

    def store(fn):
        for s in range(slabs):
            o_ref[s] = fn(y[:, s * LANES:(s + 1) * LANES]).astype(o_ref.dtype)

    def full_norm(z):
        return _rms(z) * hg_ref[...]

    def half_norm(z):
        lo = lax.broadcasted_iota(jnp.int32, z.shape, 1) < (LANES // 2)
        zz = z * z
        ss_lo = jnp.sum(jnp.where(lo, zz, 0.0), axis=-1, keepdims=True)
        ss_hi = jnp.sum(jnp.where(lo, 0.0, zz), axis=-1, keepdims=True)
        inv = jnp.where(lo, lax.rsqrt(ss_lo / (LANES // 2) + RMS_EPS),
                        lax.rsqrt(ss_hi / (LANES // 2) + RMS_EPS))
        return z * inv * hg_ref[...]

    if head_norm is None:
        store(lambda z: z)
    else:
        fn = full_norm if head_norm == "full" else half_norm
        pl.when(j < n_norm_tiles)(lambda: store(fn))
        pl.when(j >= n_norm_tiles)(lambda: store(lambda z: z))


def _norm_proj(h, gain, w, layer, head_gains=None, head_norm=None, n_norm_tiles=0, out_dtype=BF16):
    n, d = h.shape
    n_out = w.shape[2]
    n_tiles = n_out // PROJ_TN
    slabs = PROJ_TN // LANES
    if head_gains is None:
        head_gains = jnp.ones((n_tiles, 1, LANES), F32)
    kern = functools.partial(_norm_proj_kernel, head_norm=head_norm, n_norm_tiles=n_norm_tiles)
    return pl.pallas_call(
        kern,
        grid=(n // PROJ_TM, n_tiles),
        in_specs=[
            pl.BlockSpec((PROJ_TM, d), lambda i, j: (i, 0)),
            pl.BlockSpec((1, d), lambda i, j: (0, 0)),
            _layer_spec(layer, (d, PROJ_TN), lambda i, j: (0, j)),
            pl.BlockSpec((None, 1, LANES), lambda i, j: (j, 0, 0)),
        ],
        out_specs=pl.BlockSpec((slabs, PROJ_TM, LANES), lambda i, j: (j, i, 0)),
        out_shape=jax.ShapeDtypeStruct((n_out // LANES, n, LANES), out_dtype),
        scratch_shapes=[pltpu.VMEM((PROJ_TM, d), BF16)],
        compiler_params=_params(("parallel", "arbitrary")),
        name="norm_proj",
    )(h, gain.reshape(1, d), w, head_gains)


def _retention_tables():
    h = np.arange(RET_HEADS, dtype=np.float32)
    log_g = np.log(np.float32(1.0) - np.float32(2.0) ** (np.float32(-5.0) - h)).astype(np.float32)
    pos = np.arange(RET_BLOCK, dtype=np.float32)
    rel = pos[:, None] - pos[None, :]
    scale = np.float32(RET_DK ** -0.5)
    din = np.where(rel >= 0, np.exp(np.maximum(rel, 0.0)[None] * log_g[:, None, None]), 0.0)
    dq = np.exp((pos + 1.0)[None] * log_g[:, None])
    dk = np.exp((RET_BLOCK - 1.0 - pos)[None] * log_g[:, None])
    dchunk = np.exp(RET_BLOCK * log_g)
    return ((din * scale).astype(np.float32), dq.astype(np.float32)[:, :, None],
            (dk * scale).astype(np.float32)[:, :, None], tuple(float(v) for v in dchunk))


def _retention_kernel(q_ref, k_ref, v_ref, g_ref, h_ref, wo_ref, din_ref, dq_ref, dk_ref,
                      o_ref, r_ref, y_ref, *, dchunk):
    @pl.when(pl.program_id(1) == 0)
    def _():
        r_ref[...] = jnp.zeros_like(r_ref)

    qs, vs = RET_DK // LANES, RET_DV // LANES
    chains = [(bb, hh) for bb in range(RET_BATCH) for hh in range(RET_HEADS)]

    def cat(ref, bb, first, count):
        return jnp.concatenate([ref[first + s, bb] for s in range(count)], axis=-1)

    qc = [cat(q_ref, bb, hh * qs, qs) for bb, hh in chains]
    kc = [cat(k_ref, bb, hh * qs, qs) for bb, hh in chains]
    vc = [cat(v_ref, bb, hh * vs, vs) for bb, hh in chains]
    att = [(lax.dot_general(qc[i], kc[i], _NT, preferred_element_type=F32) * din_ref[hh]).astype(BF16)
           for i, (bb, hh) in enumerate(chains)]
    state = [r_ref[bb, hh] for bb, hh in chains]
    y = [jnp.dot(att[i], vc[i], preferred_element_type=F32)
         + jnp.dot(qc[i], state[i].astype(BF16), preferred_element_type=F32) * dq_ref[hh]
         for i, (bb, hh) in enumerate(chains)]
    for i, (bb, hh) in enumerate(chains):
        kd = (kc[i].astype(F32) * dk_ref[hh]).astype(BF16)
        r_ref[bb, hh] = state[i] * dchunk[hh] + lax.dot_general(kd, vc[i], _TN,
                                                                preferred_element_type=F32)
    for i, (bb, hh) in enumerate(chains):
        gc = cat(g_ref, bb, hh * vs, vs).astype(F32)
        y_ref[bb * RET_BLOCK:(bb + 1) * RET_BLOCK, hh * RET_DV:(hh + 1) * RET_DV] = (
            gc * _sigmoid(gc) * _rms(y[i])).astype(BF16)
    out = jnp.dot(y_ref[...], wo_ref[...], preferred_element_type=F32)
    for bb in range(RET_BATCH):
        o_ref[bb] = h_ref[bb] + out[bb * RET_BLOCK:(bb + 1) * RET_BLOCK]


def _retention(proj, h, w_out, layer, batch, seq):
    n, d = h.shape
    din, dq, dk, dchunk = _retention_tables()
    p4 = proj.reshape(proj.shape[0], batch, seq, LANES)
    h3 = h.reshape(batch, seq, d)
    nq = RET_HEADS * RET_DK // LANES
    nv = RET_HEADS * RET_DV // LANES
    t, nb = RET_BLOCK, RET_BATCH
    slab_spec = lambda cnt, blk: pl.BlockSpec((cnt, nb, t, LANES), lambda b, i: (blk, b, i, 0))
    const3 = lambda a: pl.BlockSpec(a.shape, lambda b, i: (0, 0, 0))
    out = pl.pallas_call(
        functools.partial(_retention_kernel, dchunk=dchunk),
        grid=(batch // nb, seq // t),
        in_specs=[
            slab_spec(nq, 0), slab_spec(nq, 1), slab_spec(nv, 1), slab_spec(nv, 2),
            pl.BlockSpec((nb, t, d), lambda b, i: (b, i, 0)),
            _layer_spec(layer, w_out.shape[1:], lambda b, i: (0, 0)),
            const3(din), const3(dq), const3(dk),
        ],
        out_specs=pl.BlockSpec((nb, t, d), lambda b, i: (b, i, 0)),
        out_shape=jax.ShapeDtypeStruct((batch, seq, d), F32),
        scratch_shapes=[pltpu.VMEM((nb, RET_HEADS, RET_DK, RET_DV), F32),
                        pltpu.VMEM((nb * t, RET_HEADS * RET_DV), BF16)],
        compiler_params=_params(("parallel", "arbitrary")),
        name="retention",
    )(p4, p4, p4, p4, h3, w_out, jnp.asarray(din), jnp.asarray(dq), jnp.asarray(dk))
    return out.reshape(n, d)


def _alibi_slopes(n):
    ratio = 2.0 ** (-8.0 / n)
    return np.array([ratio ** (i + 1) for i in range(n)], dtype=np.float32)


def _dilated_tables():
    slopes = _alibi_slopes(DIL_HEADS)
    rel = (DIL_BLOCK + np.arange(DIL_BLOCK))[:, None] - np.arange(2 * DIL_BLOCK)[None, :]
    tabs = []
    for window, dilation in DIL_GROUPS:
        valid = (rel >= 0) & (rel <= window // dilation)
        bias = -(slopes[:, None, None] * (rel * dilation).astype(np.float32)[None]) * np.float32(LOG2E)
        tabs.append(np.where(valid[None], bias, np.float32(NEG)).astype(np.float32))
    return np.stack(tabs, axis=1)


def _dilated_kernel(q0, q1, q2, k0, k1, k2, kh0, kh1, kh2, v_ref, vh_ref, tab_ref, o_ref,
                    og_ref, lg_ref):
    first_window = pl.program_id(2) == 0
    in_prev_block = lax.broadcasted_iota(jnp.int32, (DIL_BLOCK, 2 * DIL_BLOCK), 1) < DIL_BLOCK
    ones = jnp.ones((2 * DIL_BLOCK, LANES), BF16)
    for g, (q_ref, k_ref, kh_ref) in enumerate(((q0, k0, kh0), (q1, k1, kh1), (q2, k2, kh2))):
        dilation = DIL_GROUPS[g][1]
        span = DIL_BLOCK * dilation
        n_blocks = DIL_WINDOW // span
        tab = tab_ref[g]
        tab_first = jnp.where(jnp.logical_and(first_window, in_prev_block), NEG, tab)

        def rows(start):
            if dilation == 1:
                return pl.ds(start, DIL_BLOCK)
            return pl.ds(start, DIL_BLOCK, stride=dilation)

        for r in range(dilation):
            kb = [kh_ref[rows(r), :].astype(BF16)]
            vb = [vh_ref[rows(DIL_WINDOW - span + r), :].astype(BF16)]
            for blk in range(n_blocks):
                kb.append(k_ref[rows(r + blk * span), :].astype(BF16))
                vb.append(v_ref[rows(r + blk * span), :].astype(BF16))
            scores = []
            for blk in range(n_blocks):
                qb = q_ref[rows(r + blk * span), :].astype(BF16)
                keys = jnp.concatenate([kb[blk], kb[blk + 1]], axis=0)
                s = lax.dot_general(qb, keys, _NT, preferred_element_type=F32)
                scores.append(s + (tab_first if blk == 0 else tab))
            for blk in range(n_blocks):
                s = scores[blk]
                m = jnp.max(s, axis=-1, keepdims=True)
                p = jnp.exp2(s - m).astype(BF16)
                vals = jnp.concatenate([jnp.concatenate([vb[blk], vb[blk + 1]], axis=0), ones], axis=1)
                res = jnp.dot(p, vals, preferred_element_type=F32)
                den = res[:, LANES:]
                og_ref[g, rows(r + blk * span), :] = res[:, :LANES] / den
                lg_ref[g, rows(r + blk * span), :] = m + jnp.log2(den)
    l0, l1, l2 = lg_ref[0], lg_ref[1], lg_ref[2]
    m = jnp.maximum(jnp.maximum(l0, l1), l2)
    e0, e1, e2 = jnp.exp2(l0 - m), jnp.exp2(l1 - m), jnp.exp2(l2 - m)
    mix = e0 * og_ref[0] + e1 * og_ref[1] + e2 * og_ref[2]
    o_ref[...] = (mix / (e0 + e1 + e2)).astype(o_ref.dtype)


def _dilated_attention(proj, batch, seq):
    n = batch * seq
    p4 = proj.reshape(proj.shape[0], batch, seq, LANES)
    n_groups = len(DIL_GROUPS)
    v0 = 2 * n_groups * DIL_HEADS

    def cur(first):
        return pl.BlockSpec((None, None, DIL_WINDOW, LANES), lambda b, hh, w: (first + hh, b, w, 0))

    def halo(first, rows):
        per_window = DIL_WINDOW // rows
        return pl.BlockSpec((None, None, rows, LANES),
                            lambda b, hh, w: (first + hh, b, jnp.maximum(w * per_window - 1, 0), 0))

    q_specs = [cur(g * DIL_HEADS) for g in range(n_groups)]
    k_specs = [cur((n_groups + g) * DIL_HEADS) for g in range(n_groups)]
    kh_specs = [halo((n_groups + g) * DIL_HEADS, DIL_BLOCK * DIL_GROUPS[g][1]) for g in range(n_groups)]
    tabs = _dilated_tables()
    tab_spec = pl.BlockSpec((None,) + tabs.shape[1:], lambda b, hh, w: (hh, 0, 0, 0))
    o = pl.pallas_call(
        _dilated_kernel,
        grid=(batch, DIL_HEADS, seq // DIL_WINDOW),
        in_specs=q_specs + k_specs + kh_specs + [cur(v0), halo(v0, DIL_WINDOW), tab_spec],
        out_specs=pl.BlockSpec((None, None, DIL_WINDOW, LANES), lambda b, hh, w: (hh, b, w, 0)),
        out_shape=jax.ShapeDtypeStruct((DIL_HEADS, batch, seq, LANES), BF16),
        scratch_shapes=[pltpu.VMEM((n_groups, DIL_WINDOW, LANES), F32),
                        pltpu.VMEM((n_groups, DIL_WINDOW, LANES), F32)],
        compiler_params=_params(("parallel", "parallel", "arbitrary")),
        name="dilated_attention",
    )(*([p4] * 11), jnp.asarray(tabs))
    return o.reshape(DIL_HEADS, n, LANES)


def _diff_steps(seq):
    qi, ki = [], []
    for a in range(seq // DIFF_T):
        for b in range(a + 1):
            qi.append(a)
            ki.append(b)
    return np.asarray(qi, np.int32), np.asarray(ki, np.int32)


def _diff_alibi_operands():
    import ml_dtypes
    bf16 = ml_dtypes.bfloat16
    rem = (_alibi_slopes(DIFF_HEADS) * np.float32(LOG2E)).astype(np.float32)
    pieces = []
    for _ in range(3):
        piece = rem.astype(bf16).astype(np.float32)
        pieces.append(piece)
        rem = (rem - piece).astype(np.float32)
    if np.any(rem != 0):
        raise ValueError("slope * log2(e) does not split into three bf16 pieces")
    idx = np.arange(DIFF_T)
    parts = [(idx % 256).astype(np.float32), (idx - idx % 256).astype(np.float32)]
    qa = np.zeros((DIFF_HEADS, DIFF_T, LANES), np.float32)
    kb = np.zeros((DIFF_HEADS, DIFF_T, LANES), np.float32)
    col = 0
    for piece in pieces:
        for part in parts:
            qa[:, :, col] = part[None, :]
            kb[:, :, col] = -piece[:, None]
            qa[:, :, col + 1] = piece[:, None]
            kb[:, :, col + 1] = part[None, :]
            col += 2
    return qa.astype(bf16), kb.astype(bf16)


def _diff_kernel(qi_tab, ki_tab, q_ref, k_ref, v_ref, qa_ref, kb_ref, slope_ref, lq1, lk1, lq2, lk2,
                 sg_ref, o_ref, m1, l1, a1, m2, l2, a2, *, lambda_init):
    step = pl.program_id(2)
    qi = qi_tab[step]
    ki = ki_tab[step]
    slope = slope_ref[...]

    @pl.when(ki == 0)
    def _():
        for m, l, a in ((m1, l1, a1), (m2, l2, a2)):
            m[...] = jnp.full_like(m, NEG)
            l[...] = jnp.zeros_like(l)
            a[...] = jnp.zeros_like(a)

    offset = slope * ((qi - ki) * DIFF_T).astype(F32)

    q = q_ref[...]
    lo = lax.broadcasted_iota(jnp.int32, q.shape, 1) < DIFF_HEAD_DIM
    zero = jnp.zeros_like(q)
    qa = qa_ref[...]
    keys = jnp.concatenate([k_ref[...], kb_ref[...]], axis=1)
    v = v_ref[...]

    n_strips = DIFF_T // DIFF_STRIP
    units = [(s, c) for c in range(n_strips) for s in range(2)]
    stats = ((m1, l1, a1), (m2, l2, a2))

    def update(on_diagonal):
        queries = [jnp.concatenate([jnp.where(lo, q, zero), qa], axis=1),
                   jnp.concatenate([jnp.where(lo, zero, q), qa], axis=1)]

        def n_keys(c):
            return (c + 1) * DIFF_STRIP if on_diagonal else DIFF_T

        def scores(u):
            s, c = units[u]
            return lax.dot_general(keys[:n_keys(c)], queries[s][c * DIFF_STRIP:(c + 1) * DIFF_STRIP],
                                   _NT, preferred_element_type=F32)

        pending = {0: scores(0), 1: scores(1)}
        for u, (s, c) in enumerate(units):
            m_ref, l_ref, a_ref = stats[s]
            cols = slice(c * DIFF_STRIP, (c + 1) * DIFF_STRIP)
            t = pending.pop(u)
            if on_diagonal:
                key = lax.broadcasted_iota(jnp.int32, t.shape, 0)
                qry = lax.broadcasted_iota(jnp.int32, t.shape, 1) + c * DIFF_STRIP
                t = jnp.where(key <= qry, t, NEG)
            m_old = m_ref[:, cols]
            m_new = jnp.maximum(m_old, jnp.max(t, axis=0, keepdims=True) - offset[:, cols])
            alpha = jnp.exp2(m_old - m_new)
            p = jnp.exp2(t - (m_new + offset[:, cols]))
            l_ref[:, cols] = alpha * l_ref[:, cols] + jnp.sum(p, axis=0, keepdims=True)
            a_ref[:, cols] = a_ref[:, cols] * alpha + lax.dot_general(
                v[:n_keys(c)], p.astype(BF16), _TN, preferred_element_type=F32)
            m_ref[:, cols] = m_new
            if u + 2 < len(units):
                pending[u + 2] = scores(u + 2)

    pl.when(ki < qi)(lambda: update(False))
    pl.when(ki == qi)(lambda: update(True))

    @pl.when(ki == qi)
    def _():
        lam = (jnp.exp(jnp.sum(lq1[...] * lk1[...], axis=-1, keepdims=True))
               - jnp.exp(jnp.sum(lq2[...] * lk2[...], axis=-1, keepdims=True)) + lambda_init)
        o_t = a1[...] / l1[...] - lam * (a2[...] / l2[...])
        o = _rms(o_t.T) * sg_ref[...] * (1.0 - lambda_init)
        o_ref[...] = o.astype(o_ref.dtype)


def _diff_attention(proj, lq1, lk1, lq2, lk2, subln, lambda_init, batch, seq):
    n = batch * seq
    t = DIFF_T
    p4 = proj.reshape(proj.shape[0], batch, seq, LANES)
    qi_tab, ki_tab = _diff_steps(seq)
    qa, kb = _diff_alibi_operands()
    slopes = np.repeat((_alibi_slopes(DIFF_HEADS) * np.float32(LOG2E))[:, None, None], t, axis=2)
    vec = lambda a: a.reshape(1, -1).astype(F32)
    small = lambda a: pl.BlockSpec(a.shape, lambda b, hh, s, qt, kt: (0, 0))
    lqs = [vec(lq1), vec(lk1), vec(lq2), vec(lk2)]
    sg = vec(subln)
    grid_spec = pltpu.PrefetchScalarGridSpec(
        num_scalar_prefetch=2,
        grid=(batch, DIFF_HEADS, len(qi_tab)),
        in_specs=[
            pl.BlockSpec((None, None, t, LANES), lambda b, hh, s, qt, kt: (hh, b, qt[s], 0)),
            pl.BlockSpec((None, None, t, LANES), lambda b, hh, s, qt, kt: (DIFF_HEADS + hh, b, kt[s], 0)),
            pl.BlockSpec((None, None, t, LANES),
                         lambda b, hh, s, qt, kt: (2 * DIFF_HEADS + hh, b, kt[s], 0)),
            pl.BlockSpec((None, t, LANES), lambda b, hh, s, qt, kt: (hh, 0, 0)),
            pl.BlockSpec((None, t, LANES), lambda b, hh, s, qt, kt: (hh, 0, 0)),
            pl.BlockSpec((None, 1, t), lambda b, hh, s, qt, kt: (hh, 0, 0)),
            small(lqs[0]), small(lqs[1]), small(lqs[2]), small(lqs[3]), small(sg),
        ],
        out_specs=pl.BlockSpec((None, None, t, LANES), lambda b, hh, s, qt, kt: (hh, b, qt[s], 0)),
        scratch_shapes=[
            pltpu.VMEM((1, t), F32), pltpu.VMEM((1, t), F32), pltpu.VMEM((2 * DIFF_HEAD_DIM, t), F32),
            pltpu.VMEM((1, t), F32), pltpu.VMEM((1, t), F32), pltpu.VMEM((2 * DIFF_HEAD_DIM, t), F32),
        ],
    )
    o = pl.pallas_call(
        functools.partial(_diff_kernel, lambda_init=lambda_init),
        grid_spec=grid_spec,
        out_shape=jax.ShapeDtypeStruct((DIFF_HEADS, batch, seq, LANES), BF16),
        compiler_params=_params(("parallel", "parallel", "arbitrary")),
        name="diff_attention",
    )(jnp.asarray(qi_tab), jnp.asarray(ki_tab), p4, p4, p4, jnp.asarray(qa), jnp.asarray(kb),
      jnp.asarray(slopes), *lqs, sg)
    return o.reshape(DIFF_HEADS, n, LANES)


def _ffn_ple_kernel(*refs, has_mixer_out):
    if has_mixer_out:
        o_ref, wm_ref, *refs = refs
    x_ref, p_ref, g_ref, wi_ref, wo_ref, gg_ref, wg_ref, wp_ref, pg_ref, out_ref = refs
    x = x_ref[...]
    if has_mixer_out:
        o = jnp.concatenate([o_ref[s] for s in range(o_ref.shape[0])], axis=-1)
        x = x + jnp.dot(o, wm_ref[...], preferred_element_type=F32)
    xn = (_rms(x) * g_ref[...]).astype(BF16)
    h = x
    for c0 in range(0, FFN_HIDDEN, FFN_CHUNK):
        c1 = min(c0 + FFN_CHUNK, FFN_HIDDEN)
        a = jnp.dot(xn, wi_ref[:, c0:c1], preferred_element_type=F32)
        b = jnp.dot(xn, wi_ref[:, FFN_HIDDEN + c0:FFN_HIDDEN + c1], preferred_element_type=F32)
        act = (a * _sigmoid(a) * b).astype(BF16)
        h = h + jnp.dot(act, wo_ref[c0:c1, :], preferred_element_type=F32)
    hn = (_rms(h) * gg_ref[...]).astype(BF16)
    gate = _sigmoid(jnp.dot(hn, wg_ref[...], preferred_element_type=F32))
    e = jnp.dot(p_ref[...].astype(BF16), wp_ref[...], preferred_element_type=F32)
    out_ref[...] = h + gate * (_rms(e) * pg_ref[...])


def _ffn_ple(h, mixer_out, p, layer, ffn_gain, w_in, w_out, gate_gain, w_gate, w_proj, ple_gain):
    n, d = h.shape
    tm = FFN_TM
    row = lambda i: (i, 0)
    fixed = lambda i: (0, 0)
    resident = lambda w, l: _layer_spec(l, w.shape[1:], fixed, pipeline_mode=pl.Buffered(1))
    operands, specs = [], []
    if mixer_out is not None:
        o, w_mix, mix_layer = mixer_out
        operands += [o, w_mix]
        specs += [pl.BlockSpec((o.shape[0], tm, LANES), lambda i: (0, i, 0)), resident(w_mix, mix_layer)]
    operands += [h, p, ffn_gain.reshape(1, d), w_in, w_out, gate_gain.reshape(1, d), w_gate, w_proj,
                 ple_gain.reshape(1, d)]
    specs += [pl.BlockSpec((tm, d), row), _layer_spec(layer, (tm, p.shape[2]), row),
              pl.BlockSpec((1, d), fixed), resident(w_in, layer), resident(w_out, layer),
              pl.BlockSpec((1, d), fixed), resident(w_gate, layer), resident(w_proj, layer),
              pl.BlockSpec((1, d), fixed)]
    return pl.pallas_call(
        functools.partial(_ffn_ple_kernel, has_mixer_out=mixer_out is not None),
        grid=(n // tm,),
        in_specs=specs,
        out_specs=pl.BlockSpec((tm, d), row),
        out_shape=jax.ShapeDtypeStruct((n, d), F32),
        compiler_params=_params(("parallel",)),
        name="ffn_ple",
    )(*operands)


def _diff_lambda_init(layer_idx):
    return 0.8 - 0.6 * math.exp(-0.3 * layer_idx)


def _tile_gains(rows):
    return jnp.stack([jnp.tile(r, LANES // r.shape[0]) for r in rows])[:, None, :].astype(F32)


def kernel(x, p, mix_norm, ffn_norm, a_w_in, a_w_out, b_w_in, b_q_norm, b_k_norm, b_w_out,
           c_w_in, c_q_norm, c_k_norm, c_lambda_q1, c_lambda_k1, c_lambda_q2, c_lambda_k2,
           c_subln, c_w_out, ffn_w_in, ffn_w_out, ple_w_proj, ple_norm, ple_gate_norm, ple_w_gate):
    batch, seq, d = x.shape
    depth = p.shape[0]
    n = batch * seq
    h = x.reshape(n, d)
    bf = lambda w: w.astype(BF16)
    a_w_in, a_w_out, b_w_in, b_w_out, c_w_in, c_w_out = map(bf, (a_w_in, a_w_out, b_w_in, b_w_out,
                                                                 c_w_in, c_w_out))
    ffn_w_in, ffn_w_out, ple_w_gate, ple_w_proj = map(bf, (ffn_w_in, ffn_w_out, ple_w_gate, ple_w_proj))
    p3 = p.reshape(depth, n, PLE_DIM)
    ones = jnp.ones((LANES,), F32)
    for i in range(depth):
        kind, j = i % N_MIXERS, i // N_MIXERS
        mixer_out = None
        if kind == 0:
            proj = _norm_proj(h, mix_norm[i], a_w_in, j)
            h = _retention(proj, h, a_w_out, j, batch, seq)
        elif kind == 1:
            q_gain = b_q_norm[j] * (DIL_HEAD_DIM ** -0.5 * LOG2E)
            gains = _tile_gains([q_gain] * 3 + [b_k_norm[j]] * 3 + [ones])
            proj = _norm_proj(h, mix_norm[i], b_w_in, j, gains, "full", 6, out_dtype=F32)
            mixer_out = (_dilated_attention(proj, batch, seq), b_w_out, j)
        else:
            gains = _tile_gains([c_q_norm[j] * (DIFF_HEAD_DIM ** -0.5 * LOG2E), c_k_norm[j], ones])
            proj = _norm_proj(h, mix_norm[i], c_w_in, j, gains, "half", 2)
            o = _diff_attention(proj, c_lambda_q1[j], c_lambda_k1[j], c_lambda_q2[j],
                                c_lambda_k2[j], c_subln[j], _diff_lambda_init(i), batch, seq)
            mixer_out = (o, c_w_out, j)
        h = _ffn_ple(h, mixer_out, p3, i, ffn_norm[i], ffn_w_in, ffn_w_out,
                     ple_gate_norm[i], ple_w_gate, ple_w_proj, ple_norm[i])
    return h.reshape(batch, seq, d)
```

```python
import functools
import math

import numpy as np
import jax
import jax.numpy as jnp
from jax import lax
from jax.experimental import pallas as pl
from jax.experimental.pallas import tpu as pltpu

D_MODEL = 1024
PLE_DIM = 256
N_MIXERS = 3
RMS_EPS = 1e-6
NEG = -1e30

RET_HEADS = 4
RET_DK = 256
RET_DV = 512
RET_CHUNK = 128
RET_IN = 2 * RET_HEADS * RET_DK + 2 * RET_HEADS * RET_DV

DIL_GROUPS = ((128, 1), (512, 4), (2048, 16))
DIL_HEADS = 8
DIL_HEAD_DIM = 128
DIL_BLOCK = 128
DIL_QK = len(DIL_GROUPS) * DIL_HEADS * DIL_HEAD_DIM
DIL_IN = 2 * DIL_QK + DIL_HEADS * DIL_HEAD_DIM

DIFF_HEADS = 8
DIFF_HEAD_DIM = 64
DIFF_QK = 2 * DIFF_HEADS * DIFF_HEAD_DIM
DIFF_V = DIFF_HEADS * 2 * DIFF_HEAD_DIM
DIFF_IN = 2 * DIFF_QK + DIFF_V

FFN_HIDDEN = 2816

LANES = 128
V7X_VMEM_BYTES = 64 * 1024 * 1024
VMEM_LIMIT_BYTES = V7X_VMEM_BYTES - 8 * 1024 * 1024

PROJ_TM = 1024
PROJ_TN = 1024
FFN_TM = 512
FFN_CHUNK = 1024
RET_BLOCK = 256
RET_BATCH = 2
DIL_WINDOW = 2048
DIFF_T = 2048
DIFF_STRIP = 256
DIFF_AHEAD = 8

LOG2E = math.log2(math.e)

F32 = jnp.float32
BF16 = jnp.bfloat16

_NT = (((1,), (1,)), ((), ()))
_TN = (((0,), (0,)), ((), ()))


def _params(semantics):
    return pltpu.CompilerParams(dimension_semantics=semantics, vmem_limit_bytes=VMEM_LIMIT_BYTES)


def _layer_spec(layer, block, index_map, **kwargs):
    return pl.BlockSpec((None,) + tuple(block), lambda *g: (layer,) + tuple(index_map(*g)), **kwargs)


def _rms(x):
    return x * lax.rsqrt(jnp.mean(x * x, axis=-1, keepdims=True) + RMS_EPS)


def _sigmoid(x):
    return 1.0 / (1.0 + jnp.exp(-x))


def _norm_proj_kernel(x_ref, g_ref, w_ref, hg_ref, o_ref, xn_ref, *, head_norm, n_norm_tiles):
    j = pl.program_id(1)

    @pl.when(j == 0)
    def _():
        xn_ref[...] = (_rms(x_ref[...]) * g_ref[...]).astype(BF16)

    y = jnp.dot(xn_ref[...], w_ref[...], preferred_element_type=F32)
    slabs = y.shape[1] // LANES

    def store(fn):
        for s in range(slabs):
            o_ref[s] = fn(y[:, s * LANES:(s + 1) * LANES]).astype(o_ref.dtype)

    def full_norm(z):
        return _rms(z) * hg_ref[...]

    def half_norm(z):
        lo = lax.broadcasted_iota(jnp.int32, z.shape, 1) < (LANES // 2)
        zz = z * z
        ss_lo = jnp.sum(jnp.where(lo, zz, 0.0), axis=-1, keepdims=True)
        ss_hi = jnp.sum(jnp.where(lo, 0.0, zz), axis=-1, keepdims=True)
        inv = jnp.where(lo, lax.rsqrt(ss_lo / (LANES // 2) + RMS_EPS),
                        lax.rsqrt(ss_hi / (LANES // 2) + RMS_EPS))
        return z * inv * hg_ref[...]

    if head_norm is None:
        store(lambda z: z)
    else:
        fn = full_norm if head_norm == "full" else half_norm
        pl.when(j < n_norm_tiles)(lambda: store(fn))
        pl.when(j >= n_norm_tiles)(lambda: store(lambda z: z))


def _norm_proj(h, gain, w, layer, head_gains=None, head_norm=None, n_norm_tiles=0, out_dtype=BF16):
    n, d = h.shape
    n_out = w.shape[2]
    n_tiles = n_out // PROJ_TN
    slabs = PROJ_TN // LANES
    if head_gains is None:
        head_gains = jnp.ones((n_tiles, 1, LANES), F32)
    kern = functools.partial(_norm_proj_kernel, head_norm=head_norm, n_norm_tiles=n_norm_tiles)
    return pl.pallas_call(
        kern,
        grid=(n // PROJ_TM, n_tiles),
        in_specs=[
            pl.BlockSpec((PROJ_TM, d), lambda i, j: (i, 0)),
            pl.BlockSpec((1, d), lambda i, j: (0, 0)),
            _layer_spec(layer, (d, PROJ_TN), lambda i, j: (0, j)),
            pl.BlockSpec((None, 1, LANES), lambda i, j: (j, 0, 0)),
        ],
        out_specs=pl.BlockSpec((slabs, PROJ_TM, LANES), lambda i, j: (j, i, 0)),
        out_shape=jax.ShapeDtypeStruct((n_out // LANES, n, LANES), out_dtype),
        scratch_shapes=[pltpu.VMEM((PROJ_TM, d), BF16)],
        compiler_params=_params(("parallel", "arbitrary")),
        name="norm_proj",
    )(h, gain.reshape(1, d), w, head_gains)


def _retention_tables():
    h = np.arange(RET_HEADS, dtype=np.float32)
    log_g = np.log(np.float32(1.0) - np.float32(2.0) ** (np.float32(-5.0) - h)).astype(np.float32)
    pos = np.arange(RET_BLOCK, dtype=np.float32)
    rel = pos[:, None] - pos[None, :]
    scale = np.float32(RET_DK ** -0.5)
    din = np.where(rel >= 0, np.exp(np.maximum(rel, 0.0)[None] * log_g[:, None, None]), 0.0)
    dq = np.exp((pos + 1.0)[None] * log_g[:, None])
    dk = np.exp((RET_BLOCK - 1.0 - pos)[None] * log_g[:, None])
    dchunk = np.exp(RET_BLOCK * log_g)
    return ((din * scale).astype(np.float32), dq.astype(np.float32)[:, :, None],
            (dk * scale).astype(np.float32)[:, :, None], tuple(float(v) for v in dchunk))


def _retention_kernel(q_ref, k_ref, v_ref, g_ref, h_ref, wo_ref, din_ref, dq_ref, dk_ref,
                      o_ref, r_ref, y_ref, *, dchunk):
    @pl.when(pl.program_id(1) == 0)
    def _():
        r_ref[...] = jnp.zeros_like(r_ref)

    qs, vs = RET_DK // LANES, RET_DV // LANES
    chains = [(bb, hh) for bb in range(RET_BATCH) for hh in range(RET_HEADS)]

    def cat(ref, bb, first, count):
        return jnp.concatenate([ref[first + s, bb] for s in range(count)], axis=-1)

    qc = [cat(q_ref, bb, hh * qs, qs) for bb, hh in chains]
    kc = [cat(k_ref, bb, hh * qs, qs) for bb, hh in chains]
    vc = [cat(v_ref, bb, hh * vs, vs) for bb, hh in chains]
    att = [(lax.dot_general(qc[i], kc[i], _NT, preferred_element_type=F32) * din_ref[hh]).astype(BF16)
           for i, (bb, hh) in enumerate(chains)]
    state = [r_ref[bb, hh] for bb, hh in chains]
    y = [jnp.dot(att[i], vc[i], preferred_element_type=F32)
         + jnp.dot(qc[i], state[i].astype(BF16), preferred_element_type=F32) * dq_ref[hh]
         for i, (bb, hh) in enumerate(chains)]
    for i, (bb, hh) in enumerate(chains):
        kd = (kc[i].astype(F32) * dk_ref[hh]).astype(BF16)
        r_ref[bb, hh] = state[i] * dchunk[hh] + lax.dot_general(kd, vc[i], _TN,
                                                                preferred_element_type=F32)
    for i, (bb, hh) in enumerate(chains):
        gc = cat(g_ref, bb, hh * vs, vs).astype(F32)
        y_ref[bb * RET_BLOCK:(bb + 1) * RET_BLOCK, hh * RET_DV:(hh + 1) * RET_DV] = (
            gc * _sigmoid(gc) * _rms(y[i])).astype(BF16)
    out = jnp.dot(y_ref[...], wo_ref[...], preferred_element_type=F32)
    for bb in range(RET_BATCH):
        o_ref[bb] = h_ref[bb] + out[bb * RET_BLOCK:(bb + 1) * RET_BLOCK]


def _retention(proj, h, w_out, layer, batch, seq):
    n, d = h.shape
    din, dq, dk, dchunk = _retention_tables()
    p4 = proj.reshape(proj.shape[0], batch, seq, LANES)
    h3 = h.reshape(batch, seq, d)
    nq = RET_HEADS * RET_DK // LANES
    nv = RET_HEADS * RET_DV // LANES
    t, nb = RET_BLOCK, RET_BATCH
    slab_spec = lambda cnt, blk: pl.BlockSpec((cnt, nb, t, LANES), lambda b, i: (blk, b, i, 0))
    const3 = lambda a: pl.BlockSpec(a.shape, lambda b, i: (0, 0, 0))
    out = pl.pallas_call(
        functools.partial(_retention_kernel, dchunk=dchunk),
        grid=(batch // nb, seq // t),
        in_specs=[
            slab_spec(nq, 0), slab_spec(nq, 1), slab_spec(nv, 1), slab_spec(nv, 2),
            pl.BlockSpec((nb, t, d), lambda b, i: (b, i, 0)),
            _layer_spec(layer, w_out.shape[1:], lambda b, i: (0, 0)),
            const3(din), const3(dq), const3(dk),
        ],
        out_specs=pl.BlockSpec((nb, t, d), lambda b, i: (b, i, 0)),
        out_shape=jax.ShapeDtypeStruct((batch, seq, d), F32),
        scratch_shapes=[pltpu.VMEM((nb, RET_HEADS, RET_DK, RET_DV), F32),
                        pltpu.VMEM((nb * t, RET_HEADS * RET_DV), BF16)],
        compiler_params=_params(("parallel", "arbitrary")),
        name="retention",
    )(p4, p4, p4, p4, h3, w_out, jnp.asarray(din), jnp.asarray(dq), jnp.asarray(dk))
    return out.reshape(n, d)


def _alibi_slopes(n):
    ratio = 2.0 ** (-8.0 / n)
    return np.array([ratio ** (i + 1) for i in range(n)], dtype=np.float32)


def _dilated_tables():
    slopes = _alibi_slopes(DIL_HEADS)
    rel = (DIL_BLOCK + np.arange(DIL_BLOCK))[:, None] - np.arange(2 * DIL_BLOCK)[None, :]
    tabs = []
    for window, dilation in DIL_GROUPS:
        valid = (rel >= 0) & (rel <= window // dilation)
        bias = -(slopes[:, None, None] * (rel * dilation).astype(np.float32)[None]) * np.float32(LOG2E)
        tabs.append(np.where(valid[None], bias, np.float32(NEG)).astype(np.float32))
    return np.stack(tabs, axis=1)


def _dilated_kernel(q0, q1, q2, k0, k1, k2, kh0, kh1, kh2, v_ref, vh_ref, tab_ref, o_ref,
                    og_ref, lg_ref):
    first_window = pl.program_id(2) == 0
    in_prev_block = lax.broadcasted_iota(jnp.int32, (DIL_BLOCK, 2 * DIL_BLOCK), 1) < DIL_BLOCK
    ones = jnp.ones((2 * DIL_BLOCK, LANES), BF16)
    for g, (q_ref, k_ref, kh_ref) in enumerate(((q0, k0, kh0), (q1, k1, kh1), (q2, k2, kh2))):
        dilation = DIL_GROUPS[g][1]
        span = DIL_BLOCK * dilation
        n_blocks = DIL_WINDOW // span
        tab = tab_ref[g]
        tab_first = jnp.where(jnp.logical_and(first_window, in_prev_block), NEG, tab)

        def rows(start):
            if dilation == 1:
                return pl.ds(start, DIL_BLOCK)
            return pl.ds(start, DIL_BLOCK, stride=dilation)

        for r in range(dilation):
            kb = [kh_ref[rows(r), :].astype(BF16)]
            vb = [vh_ref[rows(DIL_WINDOW - span + r), :].astype(BF16)]
            for blk in range(n_blocks):
                kb.append(k_ref[rows(r + blk * span), :].astype(BF16))
                vb.append(v_ref[rows(r + blk * span), :].astype(BF16))
            scores = []
            for blk in range(n_blocks):
                qb = q_ref[rows(r + blk * span), :].astype(BF16)
                keys = jnp.concatenate([kb[blk], kb[blk + 1]], axis=0)
                s = lax.dot_general(qb, keys, _NT, preferred_element_type=F32)
                scores.append(s + (tab_first if blk == 0 else tab))
            for blk in range(n_blocks):
                s = scores[blk]
                m = jnp.max(s, axis=-1, keepdims=True)
                p = jnp.exp2(s - m).astype(BF16)
                vals = jnp.concatenate([jnp.concatenate([vb[blk], vb[blk + 1]], axis=0), ones], axis=1)
                res = jnp.dot(p, vals, preferred_element_type=F32)
                den = res[:, LANES:]
                og_ref[g, rows(r + blk * span), :] = res[:, :LANES] / den
                lg_ref[g, rows(r + blk * span), :] = m + jnp.log2(den)
    l0, l1, l2 = lg_ref[0], lg_ref[1], lg_ref[2]
    m = jnp.maximum(jnp.maximum(l0, l1), l2)
    e0, e1, e2 = jnp.exp2(l0 - m), jnp.exp2(l1 - m), jnp.exp2(l2 - m)
    mix = e0 * og_ref[0] + e1 * og_ref[1] + e2 * og_ref[2]
    o_ref[...] = (mix / (e0 + e1 + e2)).astype(o_ref.dtype)


def _dilated_attention(proj, batch, seq):
    n = batch * seq
    p4 = proj.reshape(proj.shape[0], batch, seq, LANES)
    n_groups = len(DIL_GROUPS)
    v0 = 2 * n_groups * DIL_HEADS

    def cur(first):
        return pl.BlockSpec((None, None, DIL_WINDOW, LANES), lambda b, hh, w: (first + hh, b, w, 0))

    def halo(first, rows):
        per_window = DIL_WINDOW // rows
        return pl.BlockSpec((None, None, rows, LANES),
                            lambda b, hh, w: (first + hh, b, jnp.maximum(w * per_window - 1, 0), 0))

    q_specs = [cur(g * DIL_HEADS) for g in range(n_groups)]
    k_specs = [cur((n_groups + g) * DIL_HEADS) for g in range(n_groups)]
    kh_specs = [halo((n_groups + g) * DIL_HEADS, DIL_BLOCK * DIL_GROUPS[g][1]) for g in range(n_groups)]
    tabs = _dilated_tables()
    tab_spec = pl.BlockSpec((None,) + tabs.shape[1:], lambda b, hh, w: (hh, 0, 0, 0))
    o = pl.pallas_call(
        _dilated_kernel,
        grid=(batch, DIL_HEADS, seq // DIL_WINDOW),
        in_specs=q_specs + k_specs + kh_specs + [cur(v0), halo(v0, DIL_WINDOW), tab_spec],
        out_specs=pl.BlockSpec((None, None, DIL_WINDOW, LANES), lambda b, hh, w: (hh, b, w, 0)),
        out_shape=jax.ShapeDtypeStruct((DIL_HEADS, batch, seq, LANES), BF16),
        scratch_shapes=[pltpu.VMEM((n_groups, DIL_WINDOW, LANES), F32),
                        pltpu.VMEM((n_groups, DIL_WINDOW, LANES), F32)],
        compiler_params=_params(("parallel", "parallel", "arbitrary")),
        name="dilated_attention",
    )(*([p4] * 11), jnp.asarray(tabs))
    return o.reshape(DIL_HEADS, n, LANES)


def _diff_steps(seq):
    qi, ki = [], []
    for a in range(seq // DIFF_T):
        for b in range(a + 1):
            qi.append(a)
            ki.append(b)
    return np.asarray(qi, np.int32), np.asarray(ki, np.int32)


def _diff_alibi_operands():
    import ml_dtypes
    bf16 = ml_dtypes.bfloat16
    rem = (_alibi_slopes(DIFF_HEADS) * np.float32(LOG2E)).astype(np.float32)
    pieces = []
    for _ in range(3):
        piece = rem.astype(bf16).astype(np.float32)
        pieces.append(piece)
        rem = (rem - piece).astype(np.float32)
    if np.any(rem != 0):
        raise ValueError("slope * log2(e) does not split into three bf16 pieces")
    idx = np.arange(DIFF_T)
    parts = [(idx % 256).astype(np.float32), (idx - idx % 256).astype(np.float32)]
    qa = np.zeros((DIFF_HEADS, DIFF_T, LANES), np.float32)
    kb = np.zeros((DIFF_HEADS, DIFF_T, LANES), np.float32)
    col = 0
    for piece in pieces:
        for part in parts:
            qa[:, :, col] = part[None, :]
            kb[:, :, col] = -piece[:, None]
            qa[:, :, col + 1] = piece[:, None]
            kb[:, :, col + 1] = part[None, :]
            col += 2
    return qa.astype(bf16), kb.astype(bf16)


def _diff_kernel(qi_tab, ki_tab, q_ref, k_ref, v_ref, qa_ref, kb_ref, slope_ref, lq1, lk1, lq2, lk2,
                 sg_ref, o_ref, m1, l1, a1, m2, l2, a2, *, lambda_init):
    step = pl.program_id(2)
    qi = qi_tab[step]
    ki = ki_tab[step]
    slope = slope_ref[...]

    @pl.when(ki == 0)
    def _():
        for m, l, a in ((m1, l1, a1), (m2, l2, a2)):
            m[...] = jnp.full_like(m, NEG)
            l[...] = jnp.zeros_like(l)
            a[...] = jnp.zeros_like(a)

    offset = slope * ((qi - ki) * DIFF_T).astype(F32)

    q = q_ref[...]
    lo = lax.broadcasted_iota(jnp.int32, q.shape, 1) < DIFF_HEAD_DIM
    zero = jnp.zeros_like(q)
    qa = qa_ref[...]
    keys = jnp.concatenate([k_ref[...], kb_ref[...]], axis=1)
    v = v_ref[...]

    n_strips = DIFF_T // DIFF_STRIP
    units = [(s, c) for c in range(n_strips) for s in range(2)]
    stats = ((m1, l1, a1), (m2, l2, a2))

    def update(on_diagonal):
        queries = [jnp.concatenate([jnp.where(lo, q, zero), qa], axis=1),
                   jnp.concatenate([jnp.where(lo, zero, q), qa], axis=1)]

        def n_keys(c):
            return (c + 1) * DIFF_STRIP if on_diagonal else DIFF_T

        def scores(u):
            s, c = units[u]
            return lax.dot_general(keys[:n_keys(c)], queries[s][c * DIFF_STRIP:(c + 1) * DIFF_STRIP],
                                   _NT, preferred_element_type=F32)

        pending = {u: scores(u) for u in range(DIFF_AHEAD)}
        for u, (s, c) in enumerate(units):
            m_ref, l_ref, a_ref = stats[s]
            cols = slice(c * DIFF_STRIP, (c + 1) * DIFF_STRIP)
            t = pending.pop(u)
            if on_diagonal:
                key = lax.broadcasted_iota(jnp.int32, t.shape, 0)
                qry = lax.broadcasted_iota(jnp.int32, t.shape, 1) + c * DIFF_STRIP
                t = jnp.where(key <= qry, t, NEG)
            m_old = m_ref[:, cols]
            m_new = jnp.maximum(m_old, jnp.max(t, axis=0, keepdims=True) - offset[:, cols])
            alpha = jnp.exp2(m_old - m_new)
            p = jnp.exp2(t - (m_new + offset[:, cols]))
            l_ref[:, cols] = alpha * l_ref[:, cols] + jnp.sum(p, axis=0, keepdims=True)
            a_ref[:, cols] = a_ref[:, cols] * alpha + lax.dot_general(
                v[:n_keys(c)], p.astype(BF16), _TN, preferred_element_type=F32)
            m_ref[:, cols] = m_new
            if u + DIFF_AHEAD < len(units):
                pending[u + DIFF_AHEAD] = scores(u + DIFF_AHEAD)

    pl.when(ki < qi)(lambda: update(False))
    pl.when(ki == qi)(lambda: update(True))

    @pl.when(ki == qi)
    def _():
        lam = (jnp.exp(jnp.sum(lq1[...] * lk1[...], axis=-1, keepdims=True))
               - jnp.exp(jnp.sum(lq2[...] * lk2[...], axis=-1, keepdims=True)) + lambda_init)
        o_t = a1[...] / l1[...] - lam * (a2[...] / l2[...])
        o = _rms(o_t.T) * sg_ref[...] * (1.0 - lambda_init)
        o_ref[...] = o.astype(o_ref.dtype)


def _diff_attention(proj, lq1, lk1, lq2, lk2, subln, lambda_init, batch, seq):
    n = batch * seq
    t = DIFF_T
    p4 = proj.reshape(proj.shape[0], batch, seq, LANES)
    qi_tab, ki_tab = _diff_steps(seq)
    qa, kb = _diff_alibi_operands()
    slopes = np.repeat((_alibi_slopes(DIFF_HEADS) * np.float32(LOG2E))[:, None, None], t, axis=2)
    vec = lambda a: a.reshape(1, -1).astype(F32)
    small = lambda a: pl.BlockSpec(a.shape, lambda b, hh, s, qt, kt: (0, 0))
    lqs = [vec(lq1), vec(lk1), vec(lq2), vec(lk2)]
    sg = vec(subln)
    grid_spec = pltpu.PrefetchScalarGridSpec(
        num_scalar_prefetch=2,
        grid=(batch, DIFF_HEADS, len(qi_tab)),
        in_specs=[
            pl.BlockSpec((None, None, t, LANES), lambda b, hh, s, qt, kt: (hh, b, qt[s], 0)),
            pl.BlockSpec((None, None, t, LANES), lambda b, hh, s, qt, kt: (DIFF_HEADS + hh, b, kt[s], 0)),
            pl.BlockSpec((None, None, t, LANES),
                         lambda b, hh, s, qt, kt: (2 * DIFF_HEADS + hh, b, kt[s], 0)),
            pl.BlockSpec((None, t, LANES), lambda b, hh, s, qt, kt: (hh, 0, 0)),
            pl.BlockSpec((None, t, LANES), lambda b, hh, s, qt, kt: (hh, 0, 0)),
            pl.BlockSpec((None, 1, t), lambda b, hh, s, qt, kt: (hh, 0, 0)),
            small(lqs[0]), small(lqs[1]), small(lqs[2]), small(lqs[3]), small(sg),
        ],
        out_specs=pl.BlockSpec((None, None, t, LANES), lambda b, hh, s, qt, kt: (hh, b, qt[s], 0)),
        scratch_shapes=[
            pltpu.VMEM((1, t), F32), pltpu.VMEM((1, t), F32), pltpu.VMEM((2 * DIFF_HEAD_DIM, t), F32),
            pltpu.VMEM((1, t), F32), pltpu.VMEM((1, t), F32), pltpu.VMEM((2 * DIFF_HEAD_DIM, t), F32),
        ],
    )
    o = pl.pallas_call(
        functools.partial(_diff_kernel, lambda_init=lambda_init),
        grid_spec=grid_spec,
        out_shape=jax.ShapeDtypeStruct((DIFF_HEADS, batch, seq, LANES), BF16),
        compiler_params=_params(("parallel", "parallel", "arbitrary")),
        name="diff_attention",
    )(jnp.asarray(qi_tab), jnp.asarray(ki_tab), p4, p4, p4, jnp.asarray(qa), jnp.asarray(kb),
      jnp.asarray(slopes), *lqs, sg)
    return o.reshape(DIFF_HEADS, n, LANES)


def _ffn_ple_kernel(*refs, has_mixer_out):
    if has_mixer_out:
        o_ref, wm_ref, *refs = refs
    x_ref, p_ref, g_ref, wi_ref, wo_ref, gg_ref, wg_ref, wp_ref, pg_ref, out_ref = refs
    x = x_ref[...]
    if has_mixer_out:
        o = jnp.concatenate([o_ref[s] for s in range(o_ref.shape[0])], axis=-1)
        x = x + jnp.dot(o, wm_ref[...], preferred_element_type=F32)
    xn = (_rms(x) * g_ref[...]).astype(BF16)
    h = x
    for c0 in range(0, FFN_HIDDEN, FFN_CHUNK):
        c1 = min(c0 + FFN_CHUNK, FFN_HIDDEN)
        a = jnp.dot(xn, wi_ref[:, c0:c1], preferred_element_type=F32)
        b = jnp.dot(xn, wi_ref[:, FFN_HIDDEN + c0:FFN_HIDDEN + c1], preferred_element_type=F32)
        act = (a * _sigmoid(a) * b).astype(BF16)
        h = h + jnp.dot(act, wo_ref[c0:c1, :], preferred_element_type=F32)
    hn = (_rms(h) * gg_ref[...]).astype(BF16)
    gate = _sigmoid(jnp.dot(hn, wg_ref[...], preferred_element_type=F32))
    e = jnp.dot(p_ref[...].astype(BF16), wp_ref[...], preferred_element_type=F32)
    out_ref[...] = h + gate * (_rms(e) * pg_ref[...])


def _ffn_ple(h, mixer_out, p, layer, ffn_gain, w_in, w_out, gate_gain, w_gate, w_proj, ple_gain):
    n, d = h.shape
    tm = FFN_TM
    row = lambda i: (i, 0)
    fixed = lambda i: (0, 0)
    resident = lambda w, l: _layer_spec(l, w.shape[1:], fixed, pipeline_mode=pl.Buffered(1))
    operands, specs = [], []
    if mixer_out is not None:
        o, w_mix, mix_layer = mixer_out
        operands += [o, w_mix]
        specs += [pl.BlockSpec((o.shape[0], tm, LANES), lambda i: (0, i, 0)), resident(w_mix, mix_layer)]
    operands += [h, p, ffn_gain.reshape(1, d), w_in, w_out, gate_gain.reshape(1, d), w_gate, w_proj,
                 ple_gain.reshape(1, d)]
    specs += [pl.BlockSpec((tm, d), row), _layer_spec(layer, (tm, p.shape[2]), row),
              pl.BlockSpec((1, d), fixed), resident(w_in, layer), resident(w_out, layer),
              pl.BlockSpec((1, d), fixed), resident(w_gate, layer), resident(w_proj, layer),
              pl.BlockSpec((1, d), fixed)]
    return pl.pallas_call(
        functools.partial(_ffn_ple_kernel, has_mixer_out=mixer_out is not None),
        grid=(n // tm,),
        in_specs=specs,
        out_specs=pl.BlockSpec((tm, d), row),
        out_shape=jax.ShapeDtypeStruct((n, d), F32),
        compiler_params=_params(("parallel",)),
        name="ffn_ple",
    )(*operands)


def _diff_lambda_init(layer_idx):
    return 0.8 - 0.6 * math.exp(-0.3 * layer_idx)


def _tile_gains(rows):
    return jnp.stack([jnp.tile(r, LANES // r.shape[0]) for r in rows])[:, None, :].astype(F32)


def kernel(x, p, mix_norm, ffn_norm, a_w_in, a_w_out, b_w_in, b_q_norm, b_k_norm, b_w_out,
           c_w_in, c_q_norm, c_k_norm, c_lambda_q1, c_lambda_k1, c_lambda_q2, c_lambda_k2,
           c_subln, c_w_out, ffn_w_in, ffn_w_out, ple_w_proj, ple_norm, ple_gate_norm, ple_w_gate):
    batch, seq, d = x.shape
    depth = p.shape[0]
    n = batch * seq
    h = x.reshape(n, d)
    bf = lambda w: w.astype(BF16)
    a_w_in, a_w_out, b_w_in, b_w_out, c_w_in, c_w_out = map(bf, (a_w_in, a_w_out, b_w_in, b_w_out,
                                                                 c_w_in, c_w_out))
    ffn_w_in, ffn_w_out, ple_w_gate, ple_w_proj = map(bf, (ffn_w_in, ffn_w_out, ple_w_gate, ple_w_proj))
    p3 = p.reshape(depth, n, PLE_DIM)
    ones = jnp.ones((LANES,), F32)
    for i in range(depth):
        kind, j = i % N_MIXERS, i // N_MIXERS
        mixer_out = None
        if kind == 0:
            proj = _norm_proj(h, mix_norm[i], a_w_in, j)
            h = _retention(proj, h, a_w_out, j, batch, seq)
        elif kind == 1:
            q_gain = b_q_norm[j] * (DIL_HEAD_DIM ** -0.5 * LOG2E)
            gains = _tile_gains([q_gain] * 3 + [b_k_norm[j]] * 3 + [ones])
            proj = _norm_proj(h, mix_norm[i], b_w_in, j, gains, "full", 6, out_dtype=F32)
            mixer_out = (_dilated_attention(proj, batch, seq), b_w_out, j)
        else:
            gains = _tile_gains([c_q_norm[j] * (DIFF_HEAD_DIM ** -0.5 * LOG2E), c_k_norm[j], ones])
            proj = _norm_proj(h, mix_norm[i], c_w_in, j, gains, "half", 2)
            o = _diff_attention(proj, c_lambda_q1[j], c_lambda_k1[j], c_lambda_q2[j],
                                c_lambda_k2[j], c_subln[j], _diff_lambda_init(i), batch, seq)
            mixer_out = (o, c_w_out, j)
        h = _ffn_ple(h, mixer_out, p3, i, ffn_norm[i], ffn_w_in, ffn_w_out,
                     ple_gate_norm[i], ple_w_gate, ple_w_proj, ple_norm[i])
    return h.reshape(batch, seq, d)
```

```python
import functools
import math

import numpy as np
import jax
import jax.numpy as jnp
from jax import lax
from jax.experimental import pallas as pl
from jax.experimental.pallas import tpu as pltpu

D_MODEL = 1024
PLE_DIM = 256
N_MIXERS = 3
RMS_EPS = 1e-6
NEG = -1e30

RET_HEADS = 4
RET_DK = 256
RET_DV = 512
RET_CHUNK = 128
RET_IN = 2 * RET_HEADS * RET_DK + 2 * RET_HEADS * RET_DV

DIL_GROUPS = ((128, 1), (512, 4), (2048, 16))
DIL_HEADS = 8
DIL_HEAD_DIM = 128
DIL_BLOCK = 128
DIL_QK = len(DIL_GROUPS) * DIL_HEADS * DIL_HEAD_DIM
DIL_IN = 2 * DIL_QK + DIL_HEADS * DIL_HEAD_DIM

DIFF_HEADS = 8
DIFF_HEAD_DIM = 64
DIFF_QK = 2 * DIFF_HEADS * DIFF_HEAD_DIM
DIFF_V = DIFF_HEADS * 2 * DIFF_HEAD_DIM
DIFF_IN = 2 * DIFF_QK + DIFF_V

FFN_HIDDEN = 2816

LANES = 128
V7X_VMEM_BYTES = 64 * 1024 * 1024
VMEM_LIMIT_BYTES = V7X_VMEM_BYTES - 8 * 1024 * 1024

PROJ_TM = 1024
PROJ_TN = 1024
PROJ_CHUNK = 256
PROJ_AHEAD = 2
FFN_TM = 512
FFN_CHUNK = 1024
RET_BLOCK = 256
RET_BATCH = 2
DIL_WINDOW = 2048
DIL_HOP = 4
DIFF_T = 2048
DIFF_STRIP = 256
DIFF_AHEAD = 8

LOG2E = math.log2(math.e)

F32 = jnp.float32
BF16 = jnp.bfloat16

_NT = (((1,), (1,)), ((), ()))
_TN = (((0,), (0,)), ((), ()))


def _params(semantics):
    return pltpu.CompilerParams(dimension_semantics=semantics, vmem_limit_bytes=VMEM_LIMIT_BYTES)


def _layer_spec(layer, block, index_map, **kwargs):
    return pl.BlockSpec((None,) + tuple(block), lambda *g: (layer,) + tuple(index_map(*g)), **kwargs)


def _rms(x):
    return x * lax.rsqrt(jnp.mean(x * x, axis=-1, keepdims=True) + RMS_EPS)


def _sigmoid(x):
    return 1.0 / (1.0 + jnp.exp(-x))


def _norm_proj_kernel(x_ref, g_ref, w_ref, hg_ref, o_ref, xn_ref, *, head_norm, n_norm_tiles):
    j = pl.program_id(1)

    @pl.when(j == 0)
    def _():
        xn_ref[...] = (_rms(x_ref[...]) * g_ref[...]).astype(BF16)

    def full_norm(z):
        return _rms(z) * hg_ref[...]

    def half_norm(z):
        lo = lax.broadcasted_iota(jnp.int32, z.shape, 1) < (LANES // 2)
        zz = z * z
        ss_lo = jnp.sum(jnp.where(lo, zz, 0.0), axis=-1, keepdims=True)
        ss_hi = jnp.sum(jnp.where(lo, 0.0, zz), axis=-1, keepdims=True)
        inv = jnp.where(lo, lax.rsqrt(ss_lo / (LANES // 2) + RMS_EPS),
                        lax.rsqrt(ss_hi / (LANES // 2) + RMS_EPS))
        return z * inv * hg_ref[...]

    norm = {None: None, "full": full_norm, "half": half_norm}[head_norm]
    normed_tile = j < n_norm_tiles

    def emit(c, y):
        for s in range(PROJ_CHUNK // LANES):
            z = y[:, s * LANES:(s + 1) * LANES]
            if norm is not None:
                z = jnp.where(normed_tile, norm(z), z)
            z = z.astype(BF16)
            o_ref[c * (PROJ_CHUNK // LANES) + s] = z if o_ref.dtype == BF16 else pltpu.bitcast(z, o_ref.dtype)

    xn = xn_ref[...]
    n_chunks = w_ref.shape[1] // PROJ_CHUNK
    chunk = lambda c: jnp.dot(xn, w_ref[:, c * PROJ_CHUNK:(c + 1) * PROJ_CHUNK], preferred_element_type=F32)
    pending = {c: chunk(c) for c in range(min(PROJ_AHEAD, n_chunks))}
    for c in range(n_chunks):
        y = pending.pop(c)
        if c + PROJ_AHEAD < n_chunks:
            pending[c + PROJ_AHEAD] = chunk(c + PROJ_AHEAD)
        emit(c, y)


def _norm_proj(h, gain, w, layer, head_gains=None, head_norm=None, n_norm_tiles=0, out_dtype=BF16):
    n, d = h.shape
    n_out = w.shape[2]
    n_tiles = n_out // PROJ_TN
    slabs = PROJ_TN // LANES
    if head_gains is None:
        head_gains = jnp.ones((n_tiles, 1, LANES), F32)
    pack = 1 if out_dtype == BF16 else 2
    kern = functools.partial(_norm_proj_kernel, head_norm=head_norm, n_norm_tiles=n_norm_tiles)
    return pl.pallas_call(
        kern,
        grid=(n // PROJ_TM, n_tiles),
        in_specs=[
            pl.BlockSpec((PROJ_TM, d), lambda i, j: (i, 0)),
            pl.BlockSpec((1, d), lambda i, j: (0, 0)),
            _layer_spec(layer, (d, PROJ_TN), lambda i, j: (0, j)),
            pl.BlockSpec((None, 1, LANES), lambda i, j: (j, 0, 0)),
        ],
        out_specs=pl.BlockSpec((slabs, PROJ_TM // pack, LANES), lambda i, j: (j, i, 0)),
        out_shape=jax.ShapeDtypeStruct((n_out // LANES, n // pack, LANES), out_dtype),
        scratch_shapes=[pltpu.VMEM((PROJ_TM, d), BF16)],
        compiler_params=_params(("parallel", "arbitrary")),
        name="norm_proj",
    )(h, gain.reshape(1, d), w, head_gains)


def _retention_tables():
    h = np.arange(RET_HEADS, dtype=np.float32)
    log_g = np.log(np.float32(1.0) - np.float32(2.0) ** (np.float32(-5.0) - h)).astype(np.float32)
    pos = np.arange(RET_BLOCK, dtype=np.float32)
    rel = pos[:, None] - pos[None, :]
    scale = np.float32(RET_DK ** -0.5)
    din = np.where(rel >= 0, np.exp(np.maximum(rel, 0.0)[None] * log_g[:, None, None]), 0.0)
    dq = np.exp((pos + 1.0)[None] * log_g[:, None])
    dk = np.exp((RET_BLOCK - 1.0 - pos)[None] * log_g[:, None])
    dchunk = np.exp(RET_BLOCK * log_g)
    return ((din * scale).astype(np.float32), dq.astype(np.float32)[:, :, None],
            (dk * scale).astype(np.float32)[:, :, None], tuple(float(v) for v in dchunk))


def _retention_kernel(q_ref, k_ref, v_ref, g_ref, h_ref, wo_ref, din_ref, dq_ref, dk_ref,
                      o_ref, r_ref, y_ref, *, dchunk):
    @pl.when(pl.program_id(1) == 0)
    def _():
        r_ref[...] = jnp.zeros_like(r_ref)

    qs, vs = RET_DK // LANES, RET_DV // LANES
    chains = [(bb, hh) for bb in range(RET_BATCH) for hh in range(RET_HEADS)]

    def cat(ref, bb, first, count):
        return jnp.concatenate([ref[first + s, bb] for s in range(count)], axis=-1)

    qc = [cat(q_ref, bb, hh * qs, qs) for bb, hh in chains]
    kc = [cat(k_ref, bb, hh * qs, qs) for bb, hh in chains]
    vc = [cat(v_ref, bb, hh * vs, vs) for bb, hh in chains]
    att = [(lax.dot_general(qc[i], kc[i], _NT, preferred_element_type=F32) * din_ref[hh]).astype(BF16)
           for i, (bb, hh) in enumerate(chains)]
    state = [r_ref[bb, hh] for bb, hh in chains]
    y = [jnp.dot(att[i], vc[i], preferred_element_type=F32)
         + jnp.dot(qc[i], state[i].astype(BF16), preferred_element_type=F32) * dq_ref[hh]
         for i, (bb, hh) in enumerate(chains)]
    for i, (bb, hh) in enumerate(chains):
        kd = (kc[i].astype(F32) * dk_ref[hh]).astype(BF16)
        r_ref[bb, hh] = state[i] * dchunk[hh] + lax.dot_general(kd, vc[i], _TN,
                                                                preferred_element_type=F32)
    for i, (bb, hh) in enumerate(chains):
        gc = cat(g_ref, bb, hh * vs, vs).astype(F32)
        y_ref[bb * RET_BLOCK:(bb + 1) * RET_BLOCK, hh * RET_DV:(hh + 1) * RET_DV] = (
            gc * _sigmoid(gc) * _rms(y[i])).astype(BF16)
    out = jnp.dot(y_ref[...], wo_ref[...], preferred_element_type=F32)
    for bb in range(RET_BATCH):
        o_ref[bb] = h_ref[bb] + out[bb * RET_BLOCK:(bb + 1) * RET_BLOCK]


def _retention(proj, h, w_out, layer, batch, seq):
    n, d = h.shape
    din, dq, dk, dchunk = _retention_tables()
    p4 = proj.reshape(proj.shape[0], batch, seq, LANES)
    h3 = h.reshape(batch, seq, d)
    nq = RET_HEADS * RET_DK // LANES
    nv = RET_HEADS * RET_DV // LANES
    t, nb = RET_BLOCK, RET_BATCH
    slab_spec = lambda cnt, blk: pl.BlockSpec((cnt, nb, t, LANES), lambda b, i: (blk, b, i, 0))
    const3 = lambda a: pl.BlockSpec(a.shape, lambda b, i: (0, 0, 0))
    out = pl.pallas_call(
        functools.partial(_retention_kernel, dchunk=dchunk),
        grid=(batch // nb, seq // t),
        in_specs=[
            slab_spec(nq, 0), slab_spec(nq, 1), slab_spec(nv, 1), slab_spec(nv, 2),
            pl.BlockSpec((nb, t, d), lambda b, i: (b, i, 0)),
            _layer_spec(layer, w_out.shape[1:], lambda b, i: (0, 0)),
            const3(din), const3(dq), const3(dk),
        ],
        out_specs=pl.BlockSpec((nb, t, d), lambda b, i: (b, i, 0)),
        out_shape=jax.ShapeDtypeStruct((batch, seq, d), F32),
        scratch_shapes=[pltpu.VMEM((nb, RET_HEADS, RET_DK, RET_DV), F32),
                        pltpu.VMEM((nb * t, RET_HEADS * RET_DV), BF16)],
        compiler_params=_params(("parallel", "arbitrary")),
        name="retention",
    )(p4, p4, p4, p4, h3, w_out, jnp.asarray(din), jnp.asarray(dq), jnp.asarray(dk))
    return out.reshape(n, d)


def _alibi_slopes(n):
    ratio = 2.0 ** (-8.0 / n)
    return np.array([ratio ** (i + 1) for i in range(n)], dtype=np.float32)


def _dilated_tables():
    slopes = _alibi_slopes(DIL_HEADS)
    rel = (DIL_BLOCK + np.arange(DIL_BLOCK))[:, None] - np.arange(2 * DIL_BLOCK)[None, :]
    tabs = []
    for window, dilation in DIL_GROUPS:
        valid = (rel >= 0) & (rel <= window // dilation)
        bias = -(slopes[:, None, None] * (rel * dilation).astype(np.float32)[None]) * np.float32(LOG2E)
        tabs.append(np.where(valid[None], bias, np.float32(NEG)).astype(np.float32))
    return np.stack(tabs, axis=1)


def _dilated_kernel(q0, q1, q2, k0, k1, k2, kh0, kh1, kh2, v_ref, vh_ref, tab_ref, o_ref,
                    og_ref, lg_ref, stage_ref):
    first_window = pl.program_id(2) == 0
    in_prev_block = lax.broadcasted_iota(jnp.int32, (DIL_BLOCK, 2 * DIL_BLOCK), 1) < DIL_BLOCK
    ones = jnp.ones((2 * DIL_BLOCK, LANES), BF16)

    def attend(qb, keys, vals, tab):
        s = lax.dot_general(qb, keys, _NT, preferred_element_type=F32) + tab
        m = jnp.max(s, axis=-1, keepdims=True)
        p = jnp.exp2(s - m).astype(BF16)
        res = jnp.dot(p, jnp.concatenate([vals, ones], axis=1), preferred_element_type=F32)
        den = res[:, LANES:]
        return res[:, :LANES] / den, m + jnp.log2(den)

    for g, (q_ref, k_ref, kh_ref) in enumerate(((q0, k0, kh0), (q1, k1, kh1), (q2, k2, kh2))):
        dilation = DIL_GROUPS[g][1]
        span = DIL_BLOCK * dilation
        n_blocks = DIL_WINDOW // span
        tab = tab_ref[g]
        tab_first = jnp.where(jnp.logical_and(first_window, in_prev_block), NEG, tab)
        if dilation == 1:
            streams = [lambda ref, start: pltpu.bitcast(ref[pl.ds(start // 2, DIL_BLOCK // 2), :], BF16)]
        else:
            def pair(ref, start, half):
                words = ref[pl.ds(start // 2, DIL_BLOCK, stride=dilation // 2), :]
                bits = (words << 16) if half == 0 else (words & jnp.uint32(0xFFFF0000))
                return pltpu.bitcast(bits, F32).astype(BF16)
            streams = [functools.partial(pair, half=0), functools.partial(pair, half=1)]
        two_hops = dilation > DIL_HOP
        part = DIL_WINDOW // DIL_HOP

        def out_rows(r, blk):
            if dilation == 1:
                return pl.ds(blk * span, DIL_BLOCK)
            if two_hops:
                return pl.ds((r % DIL_HOP) * part + r // DIL_HOP, DIL_BLOCK, stride=DIL_HOP)
            return pl.ds(r + blk * span, DIL_BLOCK, stride=dilation)

        for r0 in range(0, dilation, len(streams)):
            for half, load in enumerate(streams):
                r = r0 + half
                kb = [load(kh_ref, r0)] + [load(k_ref, r0 + blk * span) for blk in range(n_blocks)]
                vb = [load(vh_ref, DIL_WINDOW - span + r0)] + [load(v_ref, r0 + blk * span)
                                                                for blk in range(n_blocks)]
                for blk in range(n_blocks):
                    o, lse = attend(load(q_ref, r0 + blk * span),
                                    jnp.concatenate([kb[blk], kb[blk + 1]], axis=0),
                                    jnp.concatenate([vb[blk], vb[blk + 1]], axis=0),
                                    tab_first if blk == 0 else tab)
                    if two_hops:
                        stage_ref[0, out_rows(r, blk), :] = o
                        stage_ref[1, out_rows(r, blk), :] = lse
                    else:
                        og_ref[g, out_rows(r, blk), :] = o
                        lg_ref[g, out_rows(r, blk), :] = lse
        if two_hops:
            assert n_blocks == 1 and dilation == DIL_HOP * DIL_HOP
            for c in range(DIL_HOP):
                og_ref[g, pl.ds(c, part, stride=DIL_HOP), :] = stage_ref[0, c * part:(c + 1) * part, :]
                lg_ref[g, pl.ds(c, part, stride=DIL_HOP), :] = stage_ref[1, c * part:(c + 1) * part, :]
    l0, l1, l2 = lg_ref[0], lg_ref[1], lg_ref[2]
    m = jnp.maximum(jnp.maximum(l0, l1), l2)
    e0, e1, e2 = jnp.exp2(l0 - m), jnp.exp2(l1 - m), jnp.exp2(l2 - m)
    mix = e0 * og_ref[0] + e1 * og_ref[1] + e2 * og_ref[2]
    o_ref[...] = (mix / (e0 + e1 + e2)).astype(o_ref.dtype)


def _dilated_attention(proj, batch, seq):
    n = batch * seq
    p4 = proj.reshape(proj.shape[0], batch, seq // 2, LANES)
    n_groups = len(DIL_GROUPS)
    v0 = 2 * n_groups * DIL_HEADS

    def cur(first):
        return pl.BlockSpec((None, None, DIL_WINDOW // 2, LANES), lambda b, hh, w: (first + hh, b, w, 0))

    def halo(first, rows):
        per_window = DIL_WINDOW // rows
        return pl.BlockSpec((None, None, rows // 2, LANES),
                            lambda b, hh, w: (first + hh, b, jnp.maximum(w * per_window - 1, 0), 0))

    q_specs = [cur(g * DIL_HEADS) for g in range(n_groups)]
    k_specs = [cur((n_groups + g) * DIL_HEADS) for g in range(n_groups)]
    kh_specs = [halo((n_groups + g) * DIL_HEADS, DIL_BLOCK * DIL_GROUPS[g][1]) for g in range(n_groups)]
    tabs = _dilated_tables()
    tab_spec = pl.BlockSpec((None,) + tabs.shape[1:], lambda b, hh, w: (hh, 0, 0, 0))
    o = pl.pallas_call(
        _dilated_kernel,
        grid=(batch, DIL_HEADS, seq // DIL_WINDOW),
        in_specs=q_specs + k_specs + kh_specs + [cur(v0), halo(v0, DIL_WINDOW), tab_spec],
        out_specs=pl.BlockSpec((None, None, DIL_WINDOW, LANES), lambda b, hh, w: (hh, b, w, 0)),
        out_shape=jax.ShapeDtypeStruct((DIL_HEADS, batch, seq, LANES), BF16),
        scratch_shapes=[pltpu.VMEM((n_groups, DIL_WINDOW, LANES), F32),
                        pltpu.VMEM((n_groups, DIL_WINDOW, LANES), F32),
                        pltpu.VMEM((2, DIL_WINDOW, LANES), F32)],
        compiler_params=_params(("parallel", "parallel", "arbitrary")),
        name="dilated_attention",
    )(*([p4] * 11), jnp.asarray(tabs))
    return o.reshape(DIL_HEADS, n, LANES)


def _diff_steps(seq):
    qi, ki = [], []
    for a in range(seq // DIFF_T):
        for b in range(a + 1):
            qi.append(a)
            ki.append(b)
    return np.asarray(qi, np.int32), np.asarray(ki, np.int32)


def _diff_alibi_operands():
    import ml_dtypes
    bf16 = ml_dtypes.bfloat16
    rem = (_alibi_slopes(DIFF_HEADS) * np.float32(LOG2E)).astype(np.float32)
    pieces = []
    for _ in range(3):
        piece = rem.astype(bf16).astype(np.float32)
        pieces.append(piece)
        rem = (rem - piece).astype(np.float32)
    if np.any(rem != 0):
        raise ValueError("slope * log2(e) does not split into three bf16 pieces")
    idx = np.arange(DIFF_T)
    parts = [(idx % 256).astype(np.float32), (idx - idx % 256).astype(np.float32)]
    qa = np.zeros((DIFF_HEADS, DIFF_T, LANES), np.float32)
    kb = np.zeros((DIFF_HEADS, DIFF_T, LANES), np.float32)
    col = 0
    for piece in pieces:
        for part in parts:
            qa[:, :, col] = part[None, :]
            kb[:, :, col] = -piece[:, None]
            qa[:, :, col + 1] = piece[:, None]
            kb[:, :, col + 1] = part[None, :]
            col += 2
    return qa.astype(bf16), kb.astype(bf16)


def _diff_kernel(qi_tab, ki_tab, q_ref, k_ref, v_ref, qa_ref, kb_ref, slope_ref, lq1, lk1, lq2, lk2,
                 sg_ref, o_ref, m1, l1, a1, m2, l2, a2, *, lambda_init):
    step = pl.program_id(2)
    qi = qi_tab[step]
    ki = ki_tab[step]
    slope = slope_ref[...]

    @pl.when(ki == 0)
    def _():
        for m, l, a in ((m1, l1, a1), (m2, l2, a2)):
            m[...] = jnp.full_like(m, NEG)
            l[...] = jnp.zeros_like(l)
            a[...] = jnp.zeros_like(a)

    offset = slope * ((qi - ki) * DIFF_T).astype(F32)

    q = q_ref[...]
    lo = lax.broadcasted_iota(jnp.int32, q.shape, 1) < DIFF_HEAD_DIM
    zero = jnp.zeros_like(q)
    qa = qa_ref[...]
    keys = jnp.concatenate([k_ref[...], kb_ref[...]], axis=1)
    v = v_ref[...]

    n_strips = DIFF_T // DIFF_STRIP
    units = [(s, c) for c in range(n_strips) for s in range(2)]
    stats = ((m1, l1, a1), (m2, l2, a2))

    def update(on_diagonal):
        queries = [jnp.concatenate([jnp.where(lo, q, zero), qa], axis=1),
                   jnp.concatenate([jnp.where(lo, zero, q), qa], axis=1)]

        def n_keys(c):
            return (c + 1) * DIFF_STRIP if on_diagonal else DIFF_T

        def scores(u):
            s, c = units[u]
            return lax.dot_general(keys[:n_keys(c)], queries[s][c * DIFF_STRIP:(c + 1) * DIFF_STRIP],
                                   _NT, preferred_element_type=F32)

        pending = {u: scores(u) for u in range(DIFF_AHEAD)}
        for u, (s, c) in enumerate(units):
            m_ref, l_ref, a_ref = stats[s]
            cols = slice(c * DIFF_STRIP, (c + 1) * DIFF_STRIP)
            t = pending.pop(u)
            if on_diagonal:
                key = lax.broadcasted_iota(jnp.int32, t.shape, 0)
                qry = lax.broadcasted_iota(jnp.int32, t.shape, 1) + c * DIFF_STRIP
                t = jnp.where(key <= qry, t, NEG)
            m_old = m_ref[:, cols]
            m_new = jnp.maximum(m_old, jnp.max(t, axis=0, keepdims=True) - offset[:, cols])
            alpha = jnp.exp2(m_old - m_new)
            p = jnp.exp2(t - (m_new + offset[:, cols]))
            l_ref[:, cols] = alpha * l_ref[:, cols] + jnp.sum(p, axis=0, keepdims=True)
            a_ref[:, cols] = a_ref[:, cols] * alpha + lax.dot_general(
                v[:n_keys(c)], p.astype(BF16), _TN, preferred_element_type=F32)
            m_ref[:, cols] = m_new
            if u + DIFF_AHEAD < len(units):
                pending[u + DIFF_AHEAD] = scores(u + DIFF_AHEAD)

    pl.when(ki < qi)(lambda: update(False))
    pl.when(ki == qi)(lambda: update(True))

    @pl.when(ki == qi)
    def _():
        lam = (jnp.exp(jnp.sum(lq1[...] * lk1[...], axis=-1, keepdims=True))
               - jnp.exp(jnp.sum(lq2[...] * lk2[...], axis=-1, keepdims=True)) + lambda_init)
        o_t = a1[...] / l1[...] - lam * (a2[...] / l2[...])
        o = _rms(o_t.T) * sg_ref[...] * (1.0 - lambda_init)
        o_ref[...] = o.astype(o_ref.dtype)


def _diff_attention(proj, lq1, lk1, lq2, lk2, subln, lambda_init, batch, seq):
    n = batch * seq
    t = DIFF_T
    p4 = proj.reshape(proj.shape[0], batch, seq, LANES)
    qi_tab, ki_tab = _diff_steps(seq)
    qa, kb = _diff_alibi_operands()
    slopes = np.repeat((_alibi_slopes(DIFF_HEADS) * np.float32(LOG2E))[:, None, None], t, axis=2)
    vec = lambda a: a.reshape(1, -1).astype(F32)
    small = lambda a: pl.BlockSpec(a.shape, lambda b, hh, s, qt, kt: (0, 0))
    lqs = [vec(lq1), vec(lk1), vec(lq2), vec(lk2)]
    sg = vec(subln)
    grid_spec = pltpu.PrefetchScalarGridSpec(
        num_scalar_prefetch=2,
        grid=(batch, DIFF_HEADS, len(qi_tab)),
        in_specs=[
            pl.BlockSpec((None, None, t, LANES), lambda b, hh, s, qt, kt: (hh, b, qt[s], 0)),
            pl.BlockSpec((None, None, t, LANES), lambda b, hh, s, qt, kt: (DIFF_HEADS + hh, b, kt[s], 0)),
            pl.BlockSpec((None, None, t, LANES),
                         lambda b, hh, s, qt, kt: (2 * DIFF_HEADS + hh, b, kt[s], 0)),
            pl.BlockSpec((None, t, LANES), lambda b, hh, s, qt, kt: (hh, 0, 0)),
            pl.BlockSpec((None, t, LANES), lambda b, hh, s, qt, kt: (hh, 0, 0)),
            pl.BlockSpec((None, 1, t), lambda b, hh, s, qt, kt: (hh, 0, 0)),
            small(lqs[0]), small(lqs[1]), small(lqs[2]), small(lqs[3]), small(sg),
        ],
        out_specs=pl.BlockSpec((None, None, t, LANES), lambda b, hh, s, qt, kt: (hh, b, qt[s], 0)),
        scratch_shapes=[
            pltpu.VMEM((1, t), F32), pltpu.VMEM((1, t), F32), pltpu.VMEM((2 * DIFF_HEAD_DIM, t), F32),
            pltpu.VMEM((1, t), F32), pltpu.VMEM((1, t), F32), pltpu.VMEM((2 * DIFF_HEAD_DIM, t), F32),
        ],
    )
    o = pl.pallas_call(
        functools.partial(_diff_kernel, lambda_init=lambda_init),
        grid_spec=grid_spec,
        out_shape=jax.ShapeDtypeStruct((DIFF_HEADS, batch, seq, LANES), BF16),
        compiler_params=_params(("parallel", "parallel", "arbitrary")),
        name="diff_attention",
    )(jnp.asarray(qi_tab), jnp.asarray(ki_tab), p4, p4, p4, jnp.asarray(qa), jnp.asarray(kb),
      jnp.asarray(slopes), *lqs, sg)
    return o.reshape(DIFF_HEADS, n, LANES)


def _ffn_ple_kernel(*refs, has_mixer_out):
    if has_mixer_out:
        o_ref, wm_ref, *refs = refs
    x_ref, p_ref, g_ref, wi_ref, wo_ref, gg_ref, wg_ref, wp_ref, pg_ref, out_ref = refs
    x = x_ref[...]
    if has_mixer_out:
        o = jnp.concatenate([o_ref[s] for s in range(o_ref.shape[0])], axis=-1)
        x = x + jnp.dot(o, wm_ref[...], preferred_element_type=F32)
    xn = (_rms(x) * g_ref[...]).astype(BF16)
    h = x
    for c0 in range(0, FFN_HIDDEN, FFN_CHUNK):
        c1 = min(c0 + FFN_CHUNK, FFN_HIDDEN)
        a = jnp.dot(xn, wi_ref[:, c0:c1], preferred_element_type=F32)
        b = jnp.dot(xn, wi_ref[:, FFN_HIDDEN + c0:FFN_HIDDEN + c1], preferred_element_type=F32)
        act = (a * _sigmoid(a) * b).astype(BF16)
        h = h + jnp.dot(act, wo_ref[c0:c1, :], preferred_element_type=F32)
    hn = (_rms(h) * gg_ref[...]).astype(BF16)
    gate = _sigmoid(jnp.dot(hn, wg_ref[...], preferred_element_type=F32))
    e = jnp.dot(p_ref[...].astype(BF16), wp_ref[...], preferred_element_type=F32)
    out_ref[...] = h + gate * (_rms(e) * pg_ref[...])


def _ffn_ple(h, mixer_out, p, layer, ffn_gain, w_in, w_out, gate_gain, w_gate, w_proj, ple_gain):
    n, d = h.shape
    tm = FFN_TM
    row = lambda i: (i, 0)
    fixed = lambda i: (0, 0)
    resident = lambda w, l: _layer_spec(l, w.shape[1:], fixed, pipeline_mode=pl.Buffered(1))
    operands, specs = [], []
    if mixer_out is not None:
        o, w_mix, mix_layer = mixer_out
        operands += [o, w_mix]
        specs += [pl.BlockSpec((o.shape[0], tm, LANES), lambda i: (0, i, 0)), resident(w_mix, mix_layer)]
    operands += [h, p, ffn_gain.reshape(1, d), w_in, w_out, gate_gain.reshape(1, d), w_gate, w_proj,
                 ple_gain.reshape(1, d)]
    specs += [pl.BlockSpec((tm, d), row), _layer_spec(layer, (tm, p.shape[2]), row),
              pl.BlockSpec((1, d), fixed), resident(w_in, layer), resident(w_out, layer),
              pl.BlockSpec((1, d), fixed), resident(w_gate, layer), resident(w_proj, layer),
              pl.BlockSpec((1, d), fixed)]
    return pl.pallas_call(
        functools.partial(_ffn_ple_kernel, has_mixer_out=mixer_out is not None),
        grid=(n // tm,),
        in_specs=specs,
        out_specs=pl.BlockSpec((tm, d), row),
        out_shape=jax.ShapeDtypeStruct((n, d), F32),
        compiler_params=_params(("parallel",)),
        name="ffn_ple",
    )(*operands)


def _diff_lambda_init(layer_idx):
    return 0.8 - 0.6 * math.exp(-0.3 * layer_idx)


def _tile_gains(rows):
    return jnp.stack([jnp.tile(r, LANES // r.shape[0]) for r in rows])[:, None, :].astype(F32)


def kernel(x, p, mix_norm, ffn_norm, a_w_in, a_w_out, b_w_in, b_q_norm, b_k_norm, b_w_out,
           c_w_in, c_q_norm, c_k_norm, c_lambda_q1, c_lambda_k1, c_lambda_q2, c_lambda_k2,
           c_subln, c_w_out, ffn_w_in, ffn_w_out, ple_w_proj, ple_norm, ple_gate_norm, ple_w_gate):
    batch, seq, d = x.shape
    depth = p.shape[0]
    n = batch * seq
    h = x.reshape(n, d)
    bf = lambda w: w.astype(BF16)
    a_w_in, a_w_out, b_w_in, b_w_out, c_w_in, c_w_out = map(bf, (a_w_in, a_w_out, b_w_in, b_w_out,
                                                                 c_w_in, c_w_out))
    ffn_w_in, ffn_w_out, ple_w_gate, ple_w_proj = map(bf, (ffn_w_in, ffn_w_out, ple_w_gate, ple_w_proj))
    p3 = p.reshape(depth, n, PLE_DIM)
    ones = jnp.ones((LANES,), F32)
    for i in range(depth):
        kind, j = i % N_MIXERS, i // N_MIXERS
        mixer_out = None
        if kind == 0:
            proj = _norm_proj(h, mix_norm[i], a_w_in, j)
            h = _retention(proj, h, a_w_out, j, batch, seq)
        elif kind == 1:
            q_gain = b_q_norm[j] * (DIL_HEAD_DIM ** -0.5 * LOG2E)
            gains = _tile_gains([q_gain] * 3 + [b_k_norm[j]] * 3 + [ones])
            proj = _norm_proj(h, mix_norm[i], b_w_in, j, gains, "full", 6, out_dtype=jnp.uint32)
            mixer_out = (_dilated_attention(proj, batch, seq), b_w_out, j)
        else:
            gains = _tile_gains([c_q_norm[j] * (DIFF_HEAD_DIM ** -0.5 * LOG2E), c_k_norm[j], ones])
            proj = _norm_proj(h, mix_norm[i], c_w_in, j, gains, "half", 2)
            o = _diff_attention(proj, c_lambda_q1[j], c_lambda_k1[j], c_lambda_q2[j],
                                c_lambda_k2[j], c_subln[j], _diff_lambda_init(i), batch, seq)
            mixer_out = (o, c_w_out, j)
        h = _ffn_ple(h, mixer_out, p3, i, ffn_norm[i], ffn_w_in, ffn_w_out,
                     ple_gate_norm[i], ple_w_gate, ple_w_proj, ple_norm[i])
    return h.reshape(batch, seq, d)
```

```python
import functools
import math

import numpy as np
import jax
import jax.numpy as jnp
from jax import lax
from jax.experimental import pallas as pl
from jax.experimental.pallas import tpu as pltpu

D_MODEL = 1024
PLE_DIM = 256
N_MIXERS = 3
RMS_EPS = 1e-6
NEG = -1e30

RET_HEADS = 4
RET_DK = 256
RET_DV = 512
RET_CHUNK = 128
RET_IN = 2 * RET_HEADS * RET_DK + 2 * RET_HEADS * RET_DV

DIL_GROUPS = ((128, 1), (512, 4), (2048, 16))
DIL_HEADS = 8
DIL_HEAD_DIM = 128
DIL_BLOCK = 128
DIL_QK = len(DIL_GROUPS) * DIL_HEADS * DIL_HEAD_DIM
DIL_IN = 2 * DIL_QK + DIL_HEADS * DIL_HEAD_DIM

DIFF_HEADS = 8
DIFF_HEAD_DIM = 64
DIFF_QK = 2 * DIFF_HEADS * DIFF_HEAD_DIM
DIFF_V = DIFF_HEADS * 2 * DIFF_HEAD_DIM
DIFF_IN = 2 * DIFF_QK + DIFF_V

FFN_HIDDEN = 2816

LANES = 128
V7X_VMEM_BYTES = 64 * 1024 * 1024
VMEM_LIMIT_BYTES = V7X_VMEM_BYTES - 8 * 1024 * 1024

PROJ_TM = 1024
PROJ_TN_A = 2048
PROJ_TN_B = 1792
PROJ_TN_C = 1536
PROJ_CHUNK = 256
PROJ_AHEAD = 2
FFN_TM = 512
FFN_CHUNK = 1024
RET_BLOCK = 256
RET_BATCH = 2
DIL_WINDOW = 2048
DIL_HOP = 4
DIFF_T = 2048
DIFF_STRIP = 256
DIFF_AHEAD = 8

LOG2E = math.log2(math.e)

F32 = jnp.float32
BF16 = jnp.bfloat16

_NT = (((1,), (1,)), ((), ()))
_TN = (((0,), (0,)), ((), ()))


def _params(semantics):
    return pltpu.CompilerParams(dimension_semantics=semantics, vmem_limit_bytes=VMEM_LIMIT_BYTES)


def _layer_spec(layer, block, index_map, **kwargs):
    return pl.BlockSpec((None,) + tuple(block), lambda *g: (layer,) + tuple(index_map(*g)), **kwargs)


def _rms(x):
    return x * lax.rsqrt(jnp.mean(x * x, axis=-1, keepdims=True) + RMS_EPS)


def _sigmoid(x):
    return 1.0 / (1.0 + jnp.exp(-x))


def _norm_proj_kernel(x_ref, g_ref, w_ref, hg_ref, o_ref, xn_ref, *, head_norm, n_norm_slabs):
    j = pl.program_id(1)

    @pl.when(j == 0)
    def _():
        xn_ref[...] = (_rms(x_ref[...]) * g_ref[...]).astype(BF16)

    def full_norm(z, gain):
        return _rms(z) * gain

    def half_norm(z, gain):
        lo = lax.broadcasted_iota(jnp.int32, z.shape, 1) < (LANES // 2)
        zz = z * z
        ss_lo = jnp.sum(jnp.where(lo, zz, 0.0), axis=-1, keepdims=True)
        ss_hi = jnp.sum(jnp.where(lo, 0.0, zz), axis=-1, keepdims=True)
        inv = jnp.where(lo, lax.rsqrt(ss_lo / (LANES // 2) + RMS_EPS),
                        lax.rsqrt(ss_hi / (LANES // 2) + RMS_EPS))
        return z * inv * gain

    norm = {None: None, "full": full_norm, "half": half_norm}[head_norm]
    tile_slabs = o_ref.shape[0]

    def emit(c, y):
        for s in range(PROJ_CHUNK // LANES):
            slab = c * (PROJ_CHUNK // LANES) + s
            z = y[:, s * LANES:(s + 1) * LANES]
            if norm is not None:
                z = jnp.where(j * tile_slabs + slab < n_norm_slabs, norm(z, hg_ref[slab]), z)
            z = z.astype(BF16)
            o_ref[slab] = z if o_ref.dtype == BF16 else pltpu.bitcast(z, o_ref.dtype)

    xn = xn_ref[...]
    n_chunks = w_ref.shape[1] // PROJ_CHUNK
    chunk = lambda c: jnp.dot(xn, w_ref[:, c * PROJ_CHUNK:(c + 1) * PROJ_CHUNK], preferred_element_type=F32)
    pending = {c: chunk(c) for c in range(min(PROJ_AHEAD, n_chunks))}
    for c in range(n_chunks):
        y = pending.pop(c)
        if c + PROJ_AHEAD < n_chunks:
            pending[c + PROJ_AHEAD] = chunk(c + PROJ_AHEAD)
        emit(c, y)


def _norm_proj(h, gain, w, layer, tn, slab_gains=None, head_norm=None, n_norm_slabs=0, out_dtype=BF16):
    n, d = h.shape
    n_out = w.shape[2]
    slabs = tn // LANES
    if slab_gains is None:
        slab_gains = jnp.ones((n_out // LANES, 1, LANES), F32)
    pack = 1 if out_dtype == BF16 else 2
    kern = functools.partial(_norm_proj_kernel, head_norm=head_norm, n_norm_slabs=n_norm_slabs)
    return pl.pallas_call(
        kern,
        grid=(n // PROJ_TM, n_out // tn),
        in_specs=[
            pl.BlockSpec((PROJ_TM, d), lambda i, j: (i, 0)),
            pl.BlockSpec((1, d), lambda i, j: (0, 0)),
            _layer_spec(layer, (d, tn), lambda i, j: (0, j)),
            pl.BlockSpec((slabs, 1, LANES), lambda i, j: (j, 0, 0)),
        ],
        out_specs=pl.BlockSpec((slabs, PROJ_TM // pack, LANES), lambda i, j: (j, i, 0)),
        out_shape=jax.ShapeDtypeStruct((n_out // LANES, n // pack, LANES), out_dtype),
        scratch_shapes=[pltpu.VMEM((PROJ_TM, d), BF16)],
        compiler_params=_params(("parallel", "arbitrary")),
        name="norm_proj",
    )(h, gain.reshape(1, d), w, slab_gains)


def _retention_tables():
    h = np.arange(RET_HEADS, dtype=np.float32)
    log_g = np.log(np.float32(1.0) - np.float32(2.0) ** (np.float32(-5.0) - h)).astype(np.float32)
    pos = np.arange(RET_BLOCK, dtype=np.float32)
    rel = pos[:, None] - pos[None, :]
    scale = np.float32(RET_DK ** -0.5)
    din = np.where(rel >= 0, np.exp(np.maximum(rel, 0.0)[None] * log_g[:, None, None]), 0.0)
    dq = np.exp((pos + 1.0)[None] * log_g[:, None])
    dk = np.exp((RET_BLOCK - 1.0 - pos)[None] * log_g[:, None])
    dchunk = np.exp(RET_BLOCK * log_g)
    return ((din * scale).astype(np.float32), dq.astype(np.float32)[:, :, None],
            (dk * scale).astype(np.float32)[:, :, None], tuple(float(v) for v in dchunk))


def _retention_kernel(q_ref, k_ref, v_ref, g_ref, h_ref, wo_ref, din_ref, dq_ref, dk_ref,
                      o_ref, r_ref, *, dchunk):
    @pl.when(pl.program_id(1) == 0)
    def _():
        r_ref[...] = jnp.zeros_like(r_ref)

    qs, vs = RET_DK // LANES, RET_DV // LANES
    chains = [(bb, hh) for bb in range(RET_BATCH) for hh in range(RET_HEADS)]

    def cat(ref, bb, first, count):
        return jnp.concatenate([ref[first + s, bb] for s in range(count)], axis=-1)

    qc = [cat(q_ref, bb, hh * qs, qs) for bb, hh in chains]
    kc = [cat(k_ref, bb, hh * qs, qs) for bb, hh in chains]
    vc = [cat(v_ref, bb, hh * vs, vs) for bb, hh in chains]
    att = [(lax.dot_general(qc[i], kc[i], _NT, preferred_element_type=F32) * din_ref[hh]).astype(BF16)
           for i, (bb, hh) in enumerate(chains)]
    state = [r_ref[bb, hh] for bb, hh in chains]
    y = [jnp.dot(att[i], vc[i], preferred_element_type=F32)
         + jnp.dot(qc[i], state[i].astype(BF16), preferred_element_type=F32) * dq_ref[hh]
         for i, (bb, hh) in enumerate(chains)]
    for i, (bb, hh) in enumerate(chains):
        kd = (kc[i].astype(F32) * dk_ref[hh]).astype(BF16)
        r_ref[bb, hh] = state[i] * dchunk[hh] + lax.dot_general(kd, vc[i], _TN,
                                                                preferred_element_type=F32)
    out = [h_ref[bb] for bb in range(RET_BATCH)]
    for i, (bb, hh) in enumerate(chains):
        gc = cat(g_ref, bb, hh * vs, vs).astype(F32)
        gated = (gc * _sigmoid(gc) * _rms(y[i])).astype(BF16)
        out[bb] = out[bb] + jnp.dot(gated, wo_ref[hh * RET_DV:(hh + 1) * RET_DV, :],
                                    preferred_element_type=F32)
    for bb in range(RET_BATCH):
        o_ref[bb] = out[bb]


def _retention(proj, h, w_out, layer, batch, seq):
    n, d = h.shape
    din, dq, dk, dchunk = _retention_tables()
    p4 = proj.reshape(proj.shape[0], batch, seq, LANES)
    h3 = h.reshape(batch, seq, d)
    nq = RET_HEADS * RET_DK // LANES
    nv = RET_HEADS * RET_DV // LANES
    t, nb = RET_BLOCK, RET_BATCH
    slab_spec = lambda cnt, blk: pl.BlockSpec((cnt, nb, t, LANES), lambda b, i: (blk, b, i, 0))
    const3 = lambda a: pl.BlockSpec(a.shape, lambda b, i: (0, 0, 0))
    out = pl.pallas_call(
        functools.partial(_retention_kernel, dchunk=dchunk),
        grid=(batch // nb, seq // t),
        in_specs=[
            slab_spec(nq, 0), slab_spec(nq, 1), slab_spec(nv, 1), slab_spec(nv, 2),
            pl.BlockSpec((nb, t, d), lambda b, i: (b, i, 0)),
            _layer_spec(layer, w_out.shape[1:], lambda b, i: (0, 0)),
            const3(din), const3(dq), const3(dk),
        ],
        out_specs=pl.BlockSpec((nb, t, d), lambda b, i: (b, i, 0)),
        out_shape=jax.ShapeDtypeStruct((batch, seq, d), F32),
        scratch_shapes=[pltpu.VMEM((nb, RET_HEADS, RET_DK, RET_DV), F32)],
        compiler_params=_params(("parallel", "arbitrary")),
        name="retention",
    )(p4, p4, p4, p4, h3, w_out, jnp.asarray(din), jnp.asarray(dq), jnp.asarray(dk))
    return out.reshape(n, d)


def _alibi_slopes(n):
    ratio = 2.0 ** (-8.0 / n)
    return np.array([ratio ** (i + 1) for i in range(n)], dtype=np.float32)


def _dilated_tables():
    slopes = _alibi_slopes(DIL_HEADS)
    rel = (DIL_BLOCK + np.arange(DIL_BLOCK))[:, None] - np.arange(2 * DIL_BLOCK)[None, :]
    tabs = []
    for window, dilation in DIL_GROUPS:
        valid = (rel >= 0) & (rel <= window // dilation)
        bias = -(slopes[:, None, None] * (rel * dilation).astype(np.float32)[None]) * np.float32(LOG2E)
        tabs.append(np.where(valid[None], bias, np.float32(NEG)).astype(np.float32))
    return np.stack(tabs, axis=1)


def _dilated_kernel(q0, q1, q2, k0, k1, k2, kh0, kh1, kh2, v_ref, vh_ref, tab_ref, o_ref,
                    og_ref, lg_ref, stage_ref):
    first_window = pl.program_id(2) == 0
    in_prev_block = lax.broadcasted_iota(jnp.int32, (DIL_BLOCK, 2 * DIL_BLOCK), 1) < DIL_BLOCK
    ones = jnp.ones((2 * DIL_BLOCK, LANES), BF16)

    def attend(qb, keys, vals, tab):
        s = lax.dot_general(qb, keys, _NT, preferred_element_type=F32) + tab
        m = jnp.max(s, axis=-1, keepdims=True)
        p = jnp.exp2(s - m).astype(BF16)
        res = jnp.dot(p, jnp.concatenate([vals, ones], axis=1), preferred_element_type=F32)
        den = res[:, LANES:]
        return res[:, :LANES] / den, m + jnp.log2(den)

    for g, (q_ref, k_ref, kh_ref) in enumerate(((q0, k0, kh0), (q1, k1, kh1), (q2, k2, kh2))):
        dilation = DIL_GROUPS[g][1]
        span = DIL_BLOCK * dilation
        n_blocks = DIL_WINDOW // span
        tab = tab_ref[g]
        tab_first = jnp.where(jnp.logical_and(first_window, in_prev_block), NEG, tab)
        if dilation == 1:
            streams = [lambda ref, start: pltpu.bitcast(ref[pl.ds(start // 2, DIL_BLOCK // 2), :], BF16)]
        else:
            def pair(ref, start, half):
                words = ref[pl.ds(start // 2, DIL_BLOCK, stride=dilation // 2), :]
                bits = (words << 16) if half == 0 else (words & jnp.uint32(0xFFFF0000))
                return pltpu.bitcast(bits, F32).astype(BF16)
            streams = [functools.partial(pair, half=0), functools.partial(pair, half=1)]
        two_hops = dilation > DIL_HOP
        part = DIL_WINDOW // DIL_HOP

        def out_rows(r, blk):
            if dilation == 1:
                return pl.ds(blk * span, DIL_BLOCK)
            if two_hops:
                return pl.ds((r % DIL_HOP) * part + r // DIL_HOP, DIL_BLOCK, stride=DIL_HOP)
            return pl.ds(r + blk * span, DIL_BLOCK, stride=dilation)

        for r0 in range(0, dilation, len(streams)):
            for half, load in enumerate(streams):
                r = r0 + half
                kb = [load(kh_ref, r0)] + [load(k_ref, r0 + blk * span) for blk in range(n_blocks)]
                vb = [load(vh_ref, DIL_WINDOW - span + r0)] + [load(v_ref, r0 + blk * span)
                                                                for blk in range(n_blocks)]
                for blk in range(n_blocks):
                    o, lse = attend(load(q_ref, r0 + blk * span),
                                    jnp.concatenate([kb[blk], kb[blk + 1]], axis=0),
                                    jnp.concatenate([vb[blk], vb[blk + 1]], axis=0),
                                    tab_first if blk == 0 else tab)
                    if two_hops:
                        stage_ref[0, out_rows(r, blk), :] = o
                        stage_ref[1, out_rows(r, blk), :] = lse
                    else:
                        og_ref[g, out_rows(r, blk), :] = o
                        lg_ref[g, out_rows(r, blk), :] = lse
        if two_hops:
            assert n_blocks == 1 and dilation == DIL_HOP * DIL_HOP
            for c in range(DIL_HOP):
                og_ref[g, pl.ds(c, part, stride=DIL_HOP), :] = stage_ref[0, c * part:(c + 1) * part, :]
                lg_ref[g, pl.ds(c, part, stride=DIL_HOP), :] = stage_ref[1, c * part:(c + 1) * part, :]
    l0, l1, l2 = lg_ref[0], lg_ref[1], lg_ref[2]
    m = jnp.maximum(jnp.maximum(l0, l1), l2)
    e0, e1, e2 = jnp.exp2(l0 - m), jnp.exp2(l1 - m), jnp.exp2(l2 - m)
    mix = e0 * og_ref[0] + e1 * og_ref[1] + e2 * og_ref[2]
    o_ref[...] = (mix / (e0 + e1 + e2)).astype(o_ref.dtype)


def _dilated_attention(proj, batch, seq):
    n = batch * seq
    p4 = proj.reshape(proj.shape[0], batch, seq // 2, LANES)
    n_groups = len(DIL_GROUPS)
    v0 = 2 * n_groups * DIL_HEADS

    def cur(first):
        return pl.BlockSpec((None, None, DIL_WINDOW // 2, LANES), lambda b, hh, w: (first + hh, b, w, 0))

    def halo(first, rows):
        per_window = DIL_WINDOW // rows
        return pl.BlockSpec((None, None, rows // 2, LANES),
                            lambda b, hh, w: (first + hh, b, jnp.maximum(w * per_window - 1, 0), 0))

    q_specs = [cur(g * DIL_HEADS) for g in range(n_groups)]
    k_specs = [cur((n_groups + g) * DIL_HEADS) for g in range(n_groups)]
    kh_specs = [halo((n_groups + g) * DIL_HEADS, DIL_BLOCK * DIL_GROUPS[g][1]) for g in range(n_groups)]
    tabs = _dilated_tables()
    tab_spec = pl.BlockSpec((None,) + tabs.shape[1:], lambda b, hh, w: (hh, 0, 0, 0))
    o = pl.pallas_call(
        _dilated_kernel,
        grid=(batch, DIL_HEADS, seq // DIL_WINDOW),
        in_specs=q_specs + k_specs + kh_specs + [cur(v0), halo(v0, DIL_WINDOW), tab_spec],
        out_specs=pl.BlockSpec((None, None, DIL_WINDOW, LANES), lambda b, hh, w: (hh, b, w, 0)),
        out_shape=jax.ShapeDtypeStruct((DIL_HEADS, batch, seq, LANES), BF16),
        scratch_shapes=[pltpu.VMEM((n_groups, DIL_WINDOW, LANES), F32),
                        pltpu.VMEM((n_groups, DIL_WINDOW, LANES), F32),
                        pltpu.VMEM((2, DIL_WINDOW, LANES), F32)],
        compiler_params=_params(("parallel", "parallel", "arbitrary")),
        name="dilated_attention",
    )(*([p4] * 11), jnp.asarray(tabs))
    return o.reshape(DIL_HEADS, n, LANES)


def _diff_steps(seq):
    qi, ki = [], []
    for a in range(seq // DIFF_T):
        for b in range(a + 1):
            qi.append(a)
            ki.append(b)
    return np.asarray(qi, np.int32), np.asarray(ki, np.int32)


def _diff_alibi_operands():
    import ml_dtypes
    bf16 = ml_dtypes.bfloat16
    rem = (_alibi_slopes(DIFF_HEADS) * np.float32(LOG2E)).astype(np.float32)
    pieces = []
    for _ in range(3):
        piece = rem.astype(bf16).astype(np.float32)
        pieces.append(piece)
        rem = (rem - piece).astype(np.float32)
    if np.any(rem != 0):
        raise ValueError("slope * log2(e) does not split into three bf16 pieces")
    idx = np.arange(DIFF_T)
    parts = [(idx % 256).astype(np.float32), (idx - idx % 256).astype(np.float32)]
    qa = np.zeros((DIFF_HEADS, DIFF_T, LANES), np.float32)
    kb = np.zeros((DIFF_HEADS, DIFF_T, LANES), np.float32)
    col = 0
    for piece in pieces:
        for part in parts:
            qa[:, :, col] = part[None, :]
            kb[:, :, col] = -piece[:, None]
            qa[:, :, col + 1] = piece[:, None]
            kb[:, :, col + 1] = part[None, :]
            col += 2
    return qa.astype(bf16), kb.astype(bf16)


def _diff_kernel(qi_tab, ki_tab, q_ref, k_ref, v_ref, qa_ref, kb_ref, slope_ref, lq1, lk1, lq2, lk2,
                 sg_ref, o_ref, m1, l1, a1, m2, l2, a2, *, lambda_init):
    step = pl.program_id(2)
    qi = qi_tab[step]
    ki = ki_tab[step]
    slope = slope_ref[...]

    @pl.when(ki == 0)
    def _():
        for m, l, a in ((m1, l1, a1), (m2, l2, a2)):
            m[...] = jnp.full_like(m, NEG)
            l[...] = jnp.zeros_like(l)
            a[...] = jnp.zeros_like(a)

    offset = slope * ((qi - ki) * DIFF_T).astype(F32)

    q = q_ref[...]
    lo = lax.broadcasted_iota(jnp.int32, q.shape, 1) < DIFF_HEAD_DIM
    zero = jnp.zeros_like(q)
    qa = qa_ref[...]
    keys = jnp.concatenate([k_ref[...], kb_ref[...]], axis=1)
    v = v_ref[...]

    n_strips = DIFF_T // DIFF_STRIP
    units = [(s, c) for c in range(n_strips) for s in range(2)]
    stats = ((m1, l1, a1), (m2, l2, a2))

    def update(on_diagonal):
        queries = [jnp.concatenate([jnp.where(lo, q, zero), qa], axis=1),
                   jnp.concatenate([jnp.where(lo, zero, q), qa], axis=1)]

        def n_keys(c):
            return (c + 1) * DIFF_STRIP if on_diagonal else DIFF_T

        def scores(u):
            s, c = units[u]
            return lax.dot_general(keys[:n_keys(c)], queries[s][c * DIFF_STRIP:(c + 1) * DIFF_STRIP],
                                   _NT, preferred_element_type=F32)

        pending = {u: scores(u) for u in range(DIFF_AHEAD)}
        for u, (s, c) in enumerate(units):
            m_ref, l_ref, a_ref = stats[s]
            cols = slice(c * DIFF_STRIP, (c + 1) * DIFF_STRIP)
            t = pending.pop(u)
            if on_diagonal:
                key = lax.broadcasted_iota(jnp.int32, t.shape, 0)
                qry = lax.broadcasted_iota(jnp.int32, t.shape, 1) + c * DIFF_STRIP
                t = jnp.where(key <= qry, t, NEG)
            m_old = m_ref[:, cols]
            m_new = jnp.maximum(m_old, jnp.max(t, axis=0, keepdims=True) - offset[:, cols])
            alpha = jnp.exp2(m_old - m_new)
            p = jnp.exp2(t - (m_new + offset[:, cols]))
            l_ref[:, cols] = alpha * l_ref[:, cols] + jnp.sum(p, axis=0, keepdims=True)
            a_ref[:, cols] = a_ref[:, cols] * alpha + lax.dot_general(
                v[:n_keys(c)], p.astype(BF16), _TN, preferred_element_type=F32)
            m_ref[:, cols] = m_new
            if u + DIFF_AHEAD < len(units):
                pending[u + DIFF_AHEAD] = scores(u + DIFF_AHEAD)

    pl.when(ki < qi)(lambda: update(False))
    pl.when(ki == qi)(lambda: update(True))

    @pl.when(ki == qi)
    def _():
        lam = (jnp.exp(jnp.sum(lq1[...] * lk1[...], axis=-1, keepdims=True))
               - jnp.exp(jnp.sum(lq2[...] * lk2[...], axis=-1, keepdims=True)) + lambda_init)
        o_t = a1[...] / l1[...] - lam * (a2[...] / l2[...])
        o = _rms(o_t.T) * sg_ref[...] * (1.0 - lambda_init)
        o_ref[...] = o.astype(o_ref.dtype)


def _diff_attention(proj, lq1, lk1, lq2, lk2, subln, lambda_init, batch, seq):
    n = batch * seq
    t = DIFF_T
    p4 = proj.reshape(proj.shape[0], batch, seq, LANES)
    qi_tab, ki_tab = _diff_steps(seq)
    qa, kb = _diff_alibi_operands()
    slopes = np.repeat((_alibi_slopes(DIFF_HEADS) * np.float32(LOG2E))[:, None, None], t, axis=2)
    vec = lambda a: a.reshape(1, -1).astype(F32)
    small = lambda a: pl.BlockSpec(a.shape, lambda b, hh, s, qt, kt: (0, 0))
    lqs = [vec(lq1), vec(lk1), vec(lq2), vec(lk2)]
    sg = vec(subln)
    grid_spec = pltpu.PrefetchScalarGridSpec(
        num_scalar_prefetch=2,
        grid=(batch, DIFF_HEADS, len(qi_tab)),
        in_specs=[
            pl.BlockSpec((None, None, t, LANES), lambda b, hh, s, qt, kt: (hh, b, qt[s], 0)),
            pl.BlockSpec((None, None, t, LANES), lambda b, hh, s, qt, kt: (DIFF_HEADS + hh, b, kt[s], 0)),
            pl.BlockSpec((None, None, t, LANES),
                         lambda b, hh, s, qt, kt: (2 * DIFF_HEADS + hh, b, kt[s], 0)),
            pl.BlockSpec((None, t, LANES), lambda b, hh, s, qt, kt: (hh, 0, 0)),
            pl.BlockSpec((None, t, LANES), lambda b, hh, s, qt, kt: (hh, 0, 0)),
            pl.BlockSpec((None, 1, t), lambda b, hh, s, qt, kt: (hh, 0, 0)),
            small(lqs[0]), small(lqs[1]), small(lqs[2]), small(lqs[3]), small(sg),
        ],
        out_specs=pl.BlockSpec((None, None, t, LANES), lambda b, hh, s, qt, kt: (hh, b, qt[s], 0)),
        scratch_shapes=[
            pltpu.VMEM((1, t), F32), pltpu.VMEM((1, t), F32), pltpu.VMEM((2 * DIFF_HEAD_DIM, t), F32),
            pltpu.VMEM((1, t), F32), pltpu.VMEM((1, t), F32), pltpu.VMEM((2 * DIFF_HEAD_DIM, t), F32),
        ],
    )
    o = pl.pallas_call(
        functools.partial(_diff_kernel, lambda_init=lambda_init),
        grid_spec=grid_spec,
        out_shape=jax.ShapeDtypeStruct((DIFF_HEADS, batch, seq, LANES), BF16),
        compiler_params=_params(("parallel", "parallel", "arbitrary")),
        name="diff_attention",
    )(jnp.asarray(qi_tab), jnp.asarray(ki_tab), p4, p4, p4, jnp.asarray(qa), jnp.asarray(kb),
      jnp.asarray(slopes), *lqs, sg)
    return o.reshape(DIFF_HEADS, n, LANES)


def _ffn_ple_kernel(*refs, has_mixer_out):
    if has_mixer_out:
        o_ref, wm_ref, *refs = refs
    x_ref, p_ref, g_ref, wi_ref, wo_ref, gg_ref, wg_ref, wp_ref, pg_ref, out_ref = refs
    x = x_ref[...]
    if has_mixer_out:
        o = jnp.concatenate([o_ref[s] for s in range(o_ref.shape[0])], axis=-1)
        x = x + jnp.dot(o, wm_ref[...], preferred_element_type=F32)
    xn = (_rms(x) * g_ref[...]).astype(BF16)
    h = x
    for c0 in range(0, FFN_HIDDEN, FFN_CHUNK):
        c1 = min(c0 + FFN_CHUNK, FFN_HIDDEN)
        a = jnp.dot(xn, wi_ref[:, c0:c1], preferred_element_type=F32)
        b = jnp.dot(xn, wi_ref[:, FFN_HIDDEN + c0:FFN_HIDDEN + c1], preferred_element_type=F32)
        act = (a * _sigmoid(a) * b).astype(BF16)
        h = h + jnp.dot(act, wo_ref[c0:c1, :], preferred_element_type=F32)
    hn = (_rms(h) * gg_ref[...]).astype(BF16)
    gate = _sigmoid(jnp.dot(hn, wg_ref[...], preferred_element_type=F32))
    e = jnp.dot(p_ref[...].astype(BF16), wp_ref[...], preferred_element_type=F32)
    out_ref[...] = h + gate * (_rms(e) * pg_ref[...])


def _ffn_ple(h, mixer_out, p, layer, ffn_gain, w_in, w_out, gate_gain, w_gate, w_proj, ple_gain):
    n, d = h.shape
    tm = FFN_TM
    row = lambda i: (i, 0)
    fixed = lambda i: (0, 0)
    resident = lambda w, l: _layer_spec(l, w.shape[1:], fixed, pipeline_mode=pl.Buffered(1))
    operands, specs = [], []
    if mixer_out is not None:
        o, w_mix, mix_layer = mixer_out
        operands += [o, w_mix]
        specs += [pl.BlockSpec((o.shape[0], tm, LANES), lambda i: (0, i, 0)), resident(w_mix, mix_layer)]
    operands += [h, p, ffn_gain.reshape(1, d), w_in, w_out, gate_gain.reshape(1, d), w_gate, w_proj,
                 ple_gain.reshape(1, d)]
    specs += [pl.BlockSpec((tm, d), row), _layer_spec(layer, (tm, p.shape[2]), row),
              pl.BlockSpec((1, d), fixed), resident(w_in, layer), resident(w_out, layer),
              pl.BlockSpec((1, d), fixed), resident(w_gate, layer), resident(w_proj, layer),
              pl.BlockSpec((1, d), fixed)]
    return pl.pallas_call(
        functools.partial(_ffn_ple_kernel, has_mixer_out=mixer_out is not None),
        grid=(n // tm,),
        in_specs=specs,
        out_specs=pl.BlockSpec((tm, d), row),
        out_shape=jax.ShapeDtypeStruct((n, d), F32),
        compiler_params=_params(("parallel",)),
        name="ffn_ple",
    )(*operands)


def _diff_lambda_init(layer_idx):
    return 0.8 - 0.6 * math.exp(-0.3 * layer_idx)


def _slab_gains(groups):
    rows = [jnp.tile(g, LANES // g.shape[0]) for g, count in groups for _ in range(count)]
    return jnp.stack(rows)[:, None, :].astype(F32)


def kernel(x, p, mix_norm, ffn_norm, a_w_in, a_w_out, b_w_in, b_q_norm, b_k_norm, b_w_out,
           c_w_in, c_q_norm, c_k_norm, c_lambda_q1, c_lambda_k1, c_lambda_q2, c_lambda_k2,
           c_subln, c_w_out, ffn_w_in, ffn_w_out, ple_w_proj, ple_norm, ple_gate_norm, ple_w_gate):
    batch, seq, d = x.shape
    depth = p.shape[0]
    n = batch * seq
    h = x.reshape(n, d)
    bf = lambda w: w.astype(BF16)
    a_w_in, a_w_out, b_w_in, b_w_out, c_w_in, c_w_out = map(bf, (a_w_in, a_w_out, b_w_in, b_w_out,
                                                                 c_w_in, c_w_out))
    ffn_w_in, ffn_w_out, ple_w_gate, ple_w_proj = map(bf, (ffn_w_in, ffn_w_out, ple_w_gate, ple_w_proj))
    p3 = p.reshape(depth, n, PLE_DIM)
    ones = jnp.ones((LANES,), F32)
    for i in range(depth):
        kind, j = i % N_MIXERS, i // N_MIXERS
        mixer_out = None
        if kind == 0:
            proj = _norm_proj(h, mix_norm[i], a_w_in, j, PROJ_TN_A)
            h = _retention(proj, h, a_w_out, j, batch, seq)
        elif kind == 1:
            q_gain = b_q_norm[j] * (DIL_HEAD_DIM ** -0.5 * LOG2E)
            gains = _slab_gains([(q_gain, DIL_QK // LANES), (b_k_norm[j], DIL_QK // LANES),
                                 (ones, DIL_HEADS)])
            proj = _norm_proj(h, mix_norm[i], b_w_in, j, PROJ_TN_B, gains, "full", 2 * DIL_QK // LANES,
                              out_dtype=jnp.uint32)
            mixer_out = (_dilated_attention(proj, batch, seq), b_w_out, j)
        else:
            gains = _slab_gains([(c_q_norm[j] * (DIFF_HEAD_DIM ** -0.5 * LOG2E), DIFF_QK // LANES),
                                 (c_k_norm[j], DIFF_QK // LANES), (ones, DIFF_V // LANES)])
            proj = _norm_proj(h, mix_norm[i], c_w_in, j, PROJ_TN_C, gains, "half", 2 * DIFF_QK // LANES)
            o = _diff_attention(proj, c_lambda_q1[j], c_lambda_k1[j], c_lambda_q2[j],
                                c_lambda_k2[j], c_subln[j], _diff_lambda_init(i), batch, seq)
            mixer_out = (o, c_w_out, j)
        h = _ffn_ple(h, mixer_out, p3, i, ffn_norm[i], ffn_w_in, ffn_w_out,
                     ple_gate_norm[i], ple_w_gate, ple_w_proj, ple_norm[i])
    return h.reshape(batch, seq, d)
```

```python
import functools
import math

import numpy as np
import jax
import jax.numpy as jnp
from jax import lax
from jax.experimental import pallas as pl
from jax.experimental.pallas import tpu as pltpu

D_MODEL = 1024
PLE_DIM = 256
N_MIXERS = 3
RMS_EPS = 1e-6
NEG = -1e30

RET_HEADS = 4
RET_DK = 256
RET_DV = 512
RET_CHUNK = 128
RET_IN = 2 * RET_HEADS * RET_DK + 2 * RET_HEADS * RET_DV

DIL_GROUPS = ((128, 1), (512, 4), (2048, 16))
DIL_HEADS = 8
DIL_HEAD_DIM = 128
DIL_BLOCK = 128
DIL_QK = len(DIL_GROUPS) * DIL_HEADS * DIL_HEAD_DIM
DIL_IN = 2 * DIL_QK + DIL_HEADS * DIL_HEAD_DIM

DIFF_HEADS = 8
DIFF_HEAD_DIM = 64
DIFF_QK = 2 * DIFF_HEADS * DIFF_HEAD_DIM
DIFF_V = DIFF_HEADS * 2 * DIFF_HEAD_DIM
DIFF_IN = 2 * DIFF_QK + DIFF_V

FFN_HIDDEN = 2816

LANES = 128
V7X_VMEM_BYTES = 64 * 1024 * 1024
VMEM_LIMIT_BYTES = V7X_VMEM_BYTES - 8 * 1024 * 1024

PROJ_TM = 1024
PROJ_TN_A = 3072
PROJ_TN_B = 3584
PROJ_TN_C = 3072
PROJ_CHUNK = 256
PROJ_AHEAD = 2
FFN_TM = 512
FFN_CHUNK = 1024
RET_BLOCK = 256
RET_BATCH = 2
DIL_WINDOW = 2048
DIL_HOP = 4
DIFF_T = 2048
DIFF_STRIP = 256
DIFF_AHEAD = 8

LOG2E = math.log2(math.e)

F32 = jnp.float32
BF16 = jnp.bfloat16

_NT = (((1,), (1,)), ((), ()))
_TN = (((0,), (0,)), ((), ()))


def _params(semantics):
    return pltpu.CompilerParams(dimension_semantics=semantics, vmem_limit_bytes=VMEM_LIMIT_BYTES)


def _layer_spec(layer, block, index_map, **kwargs):
    return pl.BlockSpec((None,) + tuple(block), lambda *g: (layer,) + tuple(index_map(*g)), **kwargs)


def _rms(x):
    return x * lax.rsqrt(jnp.mean(x * x, axis=-1, keepdims=True) + RMS_EPS)


def _sigmoid(x):
    return 1.0 / (1.0 + jnp.exp(-x))


def _norm_proj_kernel(x_ref, g_ref, w_ref, hg_ref, o_ref, xn_ref, *, head_norm, n_norm_slabs):
    j = pl.program_id(1)

    @pl.when(j == 0)
    def _():
        xn_ref[...] = (_rms(x_ref[...]) * g_ref[...]).astype(BF16)

    def full_norm(z, gain):
        return _rms(z) * gain

    def half_norm(z, gain):
        lo = lax.broadcasted_iota(jnp.int32, z.shape, 1) < (LANES // 2)
        zz = z * z
        ss_lo = jnp.sum(jnp.where(lo, zz, 0.0), axis=-1, keepdims=True)
        ss_hi = jnp.sum(jnp.where(lo, 0.0, zz), axis=-1, keepdims=True)
        inv = jnp.where(lo, lax.rsqrt(ss_lo / (LANES // 2) + RMS_EPS),
                        lax.rsqrt(ss_hi / (LANES // 2) + RMS_EPS))
        return z * inv * gain

    norm = {None: None, "full": full_norm, "half": half_norm}[head_norm]
    tile_slabs = o_ref.shape[0]

    def emit(c, y):
        for s in range(PROJ_CHUNK // LANES):
            slab = c * (PROJ_CHUNK // LANES) + s
            z = y[:, s * LANES:(s + 1) * LANES]
            if norm is not None:
                z = jnp.where(j * tile_slabs + slab < n_norm_slabs, norm(z, hg_ref[slab]), z)
            z = z.astype(BF16)
            o_ref[slab] = z if o_ref.dtype == BF16 else pltpu.bitcast(z, o_ref.dtype)

    xn = xn_ref[...]
    n_chunks = w_ref.shape[1] // PROJ_CHUNK
    chunk = lambda c: jnp.dot(xn, w_ref[:, c * PROJ_CHUNK:(c + 1) * PROJ_CHUNK], preferred_element_type=F32)
    pending = {c: chunk(c) for c in range(min(PROJ_AHEAD, n_chunks))}
    for c in range(n_chunks):
        y = pending.pop(c)
        if c + PROJ_AHEAD < n_chunks:
            pending[c + PROJ_AHEAD] = chunk(c + PROJ_AHEAD)
        emit(c, y)


def _norm_proj(h, gain, w, layer, tn, slab_gains=None, head_norm=None, n_norm_slabs=0, out_dtype=BF16):
    n, d = h.shape
    n_out = w.shape[2]
    slabs = tn // LANES
    if slab_gains is None:
        slab_gains = jnp.ones((n_out // LANES, 1, LANES), F32)
    pack = 1 if out_dtype == BF16 else 2
    kern = functools.partial(_norm_proj_kernel, head_norm=head_norm, n_norm_slabs=n_norm_slabs)
    return pl.pallas_call(
        kern,
        grid=(n // PROJ_TM, n_out // tn),
        in_specs=[
            pl.BlockSpec((PROJ_TM, d), lambda i, j: (i, 0)),
            pl.BlockSpec((1, d), lambda i, j: (0, 0)),
            _layer_spec(layer, (d, tn), lambda i, j: (0, j)),
            pl.BlockSpec((slabs, 1, LANES), lambda i, j: (j, 0, 0)),
        ],
        out_specs=pl.BlockSpec((slabs, PROJ_TM // pack, LANES), lambda i, j: (j, i, 0)),
        out_shape=jax.ShapeDtypeStruct((n_out // LANES, n // pack, LANES), out_dtype),
        scratch_shapes=[pltpu.VMEM((PROJ_TM, d), BF16)],
        compiler_params=_params(("parallel", "arbitrary")),
        name="norm_proj",
    )(h, gain.reshape(1, d), w, slab_gains)


def _retention_tables():
    h = np.arange(RET_HEADS, dtype=np.float32)
    log_g = np.log(np.float32(1.0) - np.float32(2.0) ** (np.float32(-5.0) - h)).astype(np.float32)
    pos = np.arange(RET_BLOCK, dtype=np.float32)
    rel = pos[:, None] - pos[None, :]
    scale = np.float32(RET_DK ** -0.5)
    din = np.where(rel >= 0, np.exp(np.maximum(rel, 0.0)[None] * log_g[:, None, None]), 0.0)
    dq = np.exp((pos + 1.0)[None] * log_g[:, None])
    dk = np.exp((RET_BLOCK - 1.0 - pos)[None] * log_g[:, None])
    dchunk = np.exp(RET_BLOCK * log_g)
    return ((din * scale).astype(np.float32), dq.astype(np.float32)[:, :, None],
            (dk * scale).astype(np.float32)[:, :, None], tuple(float(v) for v in dchunk))


def _retention_kernel(q_ref, k_ref, v_ref, g_ref, h_ref, wo_ref, din_ref, dq_ref, dk_ref,
                      o_ref, r_ref, *, dchunk):
    @pl.when(pl.program_id(1) == 0)
    def _():
        r_ref[...] = jnp.zeros_like(r_ref)

    qs, vs = RET_DK // LANES, RET_DV // LANES
    chains = [(bb, hh) for bb in range(RET_BATCH) for hh in range(RET_HEADS)]

    def cat(ref, bb, first, count):
        return jnp.concatenate([ref[first + s, bb] for s in range(count)], axis=-1)

    qc = [cat(q_ref, bb, hh * qs, qs) for bb, hh in chains]
    kc = [cat(k_ref, bb, hh * qs, qs) for bb, hh in chains]
    vc = [cat(v_ref, bb, hh * vs, vs) for bb, hh in chains]
    att = [(lax.dot_general(qc[i], kc[i], _NT, preferred_element_type=F32) * din_ref[hh]).astype(BF16)
           for i, (bb, hh) in enumerate(chains)]
    state = [r_ref[bb, hh] for bb, hh in chains]
    y = [jnp.dot(att[i], vc[i], preferred_element_type=F32)
         + jnp.dot(qc[i], state[i].astype(BF16), preferred_element_type=F32) * dq_ref[hh]
         for i, (bb, hh) in enumerate(chains)]
    for i, (bb, hh) in enumerate(chains):
        kd = (kc[i].astype(F32) * dk_ref[hh]).astype(BF16)
        r_ref[bb, hh] = state[i] * dchunk[hh] + lax.dot_general(kd, vc[i], _TN,
                                                                preferred_element_type=F32)
    out = [h_ref[bb] for bb in range(RET_BATCH)]
    for i, (bb, hh) in enumerate(chains):
        gc = cat(g_ref, bb, hh * vs, vs).astype(F32)
        gated = (gc * _sigmoid(gc) * _rms(y[i])).astype(BF16)
        out[bb] = out[bb] + jnp.dot(gated, wo_ref[hh * RET_DV:(hh + 1) * RET_DV, :],
                                    preferred_element_type=F32)
    for bb in range(RET_BATCH):
        o_ref[bb] = out[bb]


def _retention(proj, h, w_out, layer, batch, seq):
    n, d = h.shape
    din, dq, dk, dchunk = _retention_tables()
    p4 = proj.reshape(proj.shape[0], batch, seq, LANES)
    h3 = h.reshape(batch, seq, d)
    nq = RET_HEADS * RET_DK // LANES
    nv = RET_HEADS * RET_DV // LANES
    t, nb = RET_BLOCK, RET_BATCH
    slab_spec = lambda cnt, blk: pl.BlockSpec((cnt, nb, t, LANES), lambda b, i: (blk, b, i, 0))
    const3 = lambda a: pl.BlockSpec(a.shape, lambda b, i: (0, 0, 0))
    out = pl.pallas_call(
        functools.partial(_retention_kernel, dchunk=dchunk),
        grid=(batch // nb, seq // t),
        in_specs=[
            slab_spec(nq, 0), slab_spec(nq, 1), slab_spec(nv, 1), slab_spec(nv, 2),
            pl.BlockSpec((nb, t, d), lambda b, i: (b, i, 0)),
            _layer_spec(layer, w_out.shape[1:], lambda b, i: (0, 0)),
            const3(din), const3(dq), const3(dk),
        ],
        out_specs=pl.BlockSpec((nb, t, d), lambda b, i: (b, i, 0)),
        out_shape=jax.ShapeDtypeStruct((batch, seq, d), F32),
        scratch_shapes=[pltpu.VMEM((nb, RET_HEADS, RET_DK, RET_DV), F32)],
        compiler_params=_params(("parallel", "arbitrary")),
        name="retention",
    )(p4, p4, p4, p4, h3, w_out, jnp.asarray(din), jnp.asarray(dq), jnp.asarray(dk))
    return out.reshape(n, d)


def _alibi_slopes(n):
    ratio = 2.0 ** (-8.0 / n)
    return np.array([ratio ** (i + 1) for i in range(n)], dtype=np.float32)


def _dilated_tables():
    slopes = _alibi_slopes(DIL_HEADS)
    rel = (DIL_BLOCK + np.arange(DIL_BLOCK))[:, None] - np.arange(2 * DIL_BLOCK)[None, :]
    tabs = []
    for window, dilation in DIL_GROUPS:
        valid = (rel >= 0) & (rel <= window // dilation)
        bias = -(slopes[:, None, None] * (rel * dilation).astype(np.float32)[None]) * np.float32(LOG2E)
        tabs.append(np.where(valid[None], bias, np.float32(NEG)).astype(np.float32))
    return np.stack(tabs, axis=1)


def _dilated_kernel(q0, q1, q2, k0, k1, k2, kh0, kh1, kh2, v_ref, vh_ref, tab_ref, o_ref,
                    og_ref, lg_ref, stage_ref):
    first_window = pl.program_id(2) == 0
    in_prev_block = lax.broadcasted_iota(jnp.int32, (DIL_BLOCK, 2 * DIL_BLOCK), 1) < DIL_BLOCK
    ones = jnp.ones((2 * DIL_BLOCK, LANES), BF16)

    def attend(qb, keys, vals, tab):
        s = lax.dot_general(qb, keys, _NT, preferred_element_type=F32) + tab
        m = jnp.max(s, axis=-1, keepdims=True)
        p = jnp.exp2(s - m).astype(BF16)
        res = jnp.dot(p, jnp.concatenate([vals, ones], axis=1), preferred_element_type=F32)
        den = res[:, LANES:]
        return res[:, :LANES] / den, m + jnp.log2(den)

    for g, (q_ref, k_ref, kh_ref) in enumerate(((q0, k0, kh0), (q1, k1, kh1), (q2, k2, kh2))):
        dilation = DIL_GROUPS[g][1]
        span = DIL_BLOCK * dilation
        n_blocks = DIL_WINDOW // span
        tab = tab_ref[g]
        tab_first = jnp.where(jnp.logical_and(first_window, in_prev_block), NEG, tab)
        if dilation == 1:
            streams = [lambda ref, start: pltpu.bitcast(ref[pl.ds(start // 2, DIL_BLOCK // 2), :], BF16)]
        else:
            def pair(ref, start, half):
                words = ref[pl.ds(start // 2, DIL_BLOCK, stride=dilation // 2), :]
                bits = (words << 16) if half == 0 else (words & jnp.uint32(0xFFFF0000))
                return pltpu.bitcast(bits, F32).astype(BF16)
            streams = [functools.partial(pair, half=0), functools.partial(pair, half=1)]
        two_hops = dilation > DIL_HOP
        part = DIL_WINDOW // DIL_HOP

        def out_rows(r, blk):
            if dilation == 1:
                return pl.ds(blk * span, DIL_BLOCK)
            if two_hops:
                return pl.ds((r % DIL_HOP) * part + r // DIL_HOP, DIL_BLOCK, stride=DIL_HOP)
            return pl.ds(r + blk * span, DIL_BLOCK, stride=dilation)

        for r0 in range(0, dilation, len(streams)):
            for half, load in enumerate(streams):
                r = r0 + half
                kb = [load(kh_ref, r0)] + [load(k_ref, r0 + blk * span) for blk in range(n_blocks)]
                vb = [load(vh_ref, DIL_WINDOW - span + r0)] + [load(v_ref, r0 + blk * span)
                                                                for blk in range(n_blocks)]
                for blk in range(n_blocks):
                    o, lse = attend(load(q_ref, r0 + blk * span),
                                    jnp.concatenate([kb[blk], kb[blk + 1]], axis=0),
                                    jnp.concatenate([vb[blk], vb[blk + 1]], axis=0),
                                    tab_first if blk == 0 else tab)
                    if two_hops:
                        stage_ref[0, out_rows(r, blk), :] = o
                        stage_ref[1, out_rows(r, blk), :] = lse
                    else:
                        og_ref[g, out_rows(r, blk), :] = o
                        lg_ref[g, out_rows(r, blk), :] = lse
        if two_hops:
            assert n_blocks == 1 and dilation == DIL_HOP * DIL_HOP
            for c in range(DIL_HOP):
                og_ref[g, pl.ds(c, part, stride=DIL_HOP), :] = stage_ref[0, c * part:(c + 1) * part, :]
                lg_ref[g, pl.ds(c, part, stride=DIL_HOP), :] = stage_ref[1, c * part:(c + 1) * part, :]
    l0, l1, l2 = lg_ref[0], lg_ref[1], lg_ref[2]
    m = jnp.maximum(jnp.maximum(l0, l1), l2)
    e0, e1, e2 = jnp.exp2(l0 - m), jnp.exp2(l1 - m), jnp.exp2(l2 - m)
    mix = e0 * og_ref[0] + e1 * og_ref[1] + e2 * og_ref[2]
    o_ref[...] = (mix / (e0 + e1 + e2)).astype(o_ref.dtype)


def _dilated_attention(proj, batch, seq):
    n = batch * seq
    p4 = proj.reshape(proj.shape[0], batch, seq // 2, LANES)
    n_groups = len(DIL_GROUPS)
    v0 = 2 * n_groups * DIL_HEADS

    def cur(first):
        return pl.BlockSpec((None, None, DIL_WINDOW // 2, LANES), lambda b, hh, w: (first + hh, b, w, 0))

    def halo(first, rows):
        per_window = DIL_WINDOW // rows
        return pl.BlockSpec((None, None, rows // 2, LANES),
                            lambda b, hh, w: (first + hh, b, jnp.maximum(w * per_window - 1, 0), 0))

    q_specs = [cur(g * DIL_HEADS) for g in range(n_groups)]
    k_specs = [cur((n_groups + g) * DIL_HEADS) for g in range(n_groups)]
    kh_specs = [halo((n_groups + g) * DIL_HEADS, DIL_BLOCK * DIL_GROUPS[g][1]) for g in range(n_groups)]
    tabs = _dilated_tables()
    tab_spec = pl.BlockSpec((None,) + tabs.shape[1:], lambda b, hh, w: (hh, 0, 0, 0))
    o = pl.pallas_call(
        _dilated_kernel,
        grid=(batch, DIL_HEADS, seq // DIL_WINDOW),
        in_specs=q_specs + k_specs + kh_specs + [cur(v0), halo(v0, DIL_WINDOW), tab_spec],
        out_specs=pl.BlockSpec((None, None, DIL_WINDOW, LANES), lambda b, hh, w: (hh, b, w, 0)),
        out_shape=jax.ShapeDtypeStruct((DIL_HEADS, batch, seq, LANES), BF16),
        scratch_shapes=[pltpu.VMEM((n_groups, DIL_WINDOW, LANES), F32),
                        pltpu.VMEM((n_groups, DIL_WINDOW, LANES), F32),
                        pltpu.VMEM((2, DIL_WINDOW, LANES), F32)],
        compiler_params=_params(("parallel", "parallel", "arbitrary")),
        name="dilated_attention",
    )(*([p4] * 11), jnp.asarray(tabs))
    return o.reshape(DIL_HEADS, n, LANES)


def _diff_steps(seq):
    qi, ki = [], []
    for a in range(seq // DIFF_T):
        for b in range(a + 1):
            qi.append(a)
            ki.append(b)
    return np.asarray(qi, np.int32), np.asarray(ki, np.int32)


def _diff_alibi_operands():
    import ml_dtypes
    bf16 = ml_dtypes.bfloat16
    rem = (_alibi_slopes(DIFF_HEADS) * np.float32(LOG2E)).astype(np.float32)
    pieces = []
    for _ in range(3):
        piece = rem.astype(bf16).astype(np.float32)
        pieces.append(piece)
        rem = (rem - piece).astype(np.float32)
    if np.any(rem != 0):
        raise ValueError("slope * log2(e) does not split into three bf16 pieces")
    idx = np.arange(DIFF_T)
    parts = [(idx % 256).astype(np.float32), (idx - idx % 256).astype(np.float32)]
    qa = np.zeros((DIFF_HEADS, DIFF_T, LANES), np.float32)
    kb = np.zeros((DIFF_HEADS, DIFF_T, LANES), np.float32)
    col = 0
    for piece in pieces:
        for part in parts:
            qa[:, :, col] = part[None, :]
            kb[:, :, col] = -piece[:, None]
            qa[:, :, col + 1] = piece[:, None]
            kb[:, :, col + 1] = part[None, :]
            col += 2
    return qa.astype(bf16), kb.astype(bf16)


def _diff_kernel(qi_tab, ki_tab, q_ref, k_ref, v_ref, qa_ref, kb_ref, slope_ref, lq1, lk1, lq2, lk2,
                 sg_ref, o_ref, m1, l1, a1, m2, l2, a2, *, lambda_init):
    step = pl.program_id(2)
    qi = qi_tab[step]
    ki = ki_tab[step]
    slope = slope_ref[...]

    @pl.when(ki == 0)
    def _():
        for m, l, a in ((m1, l1, a1), (m2, l2, a2)):
            m[...] = jnp.full_like(m, NEG)
            l[...] = jnp.zeros_like(l)
            a[...] = jnp.zeros_like(a)

    offset = slope * ((qi - ki) * DIFF_T).astype(F32)

    q = q_ref[...]
    lo = lax.broadcasted_iota(jnp.int32, q.shape, 1) < DIFF_HEAD_DIM
    zero = jnp.zeros_like(q)
    qa = qa_ref[...]
    keys = jnp.concatenate([k_ref[...], kb_ref[...]], axis=1)
    v = v_ref[...]

    n_strips = DIFF_T // DIFF_STRIP
    units = [(s, c) for c in range(n_strips) for s in range(2)]
    stats = ((m1, l1, a1), (m2, l2, a2))

    def update(on_diagonal):
        queries = [jnp.concatenate([jnp.where(lo, q, zero), qa], axis=1),
                   jnp.concatenate([jnp.where(lo, zero, q), qa], axis=1)]

        def n_keys(c):
            return (c + 1) * DIFF_STRIP if on_diagonal else DIFF_T

        def scores(u):
            s, c = units[u]
            return lax.dot_general(keys[:n_keys(c)], queries[s][c * DIFF_STRIP:(c + 1) * DIFF_STRIP],
                                   _NT, preferred_element_type=F32)

        pending = {u: scores(u) for u in range(DIFF_AHEAD)}
        for u, (s, c) in enumerate(units):
            m_ref, l_ref, a_ref = stats[s]
            cols = slice(c * DIFF_STRIP, (c + 1) * DIFF_STRIP)
            t = pending.pop(u)
            if on_diagonal:
                key = lax.broadcasted_iota(jnp.int32, t.shape, 0)
                qry = lax.broadcasted_iota(jnp.int32, t.shape, 1) + c * DIFF_STRIP
                t = jnp.where(key <= qry, t, NEG)
            m_old = m_ref[:, cols]
            m_new = jnp.maximum(m_old, jnp.max(t, axis=0, keepdims=True) - offset[:, cols])
            alpha = jnp.exp2(m_old - m_new)
            p = jnp.exp2(t - (m_new + offset[:, cols]))
            l_ref[:, cols] = alpha * l_ref[:, cols] + jnp.sum(p, axis=0, keepdims=True)
            a_ref[:, cols] = a_ref[:, cols] * alpha + lax.dot_general(
                v[:n_keys(c)], p.astype(BF16), _TN, preferred_element_type=F32)
            m_ref[:, cols] = m_new
            if u + DIFF_AHEAD < len(units):
                pending[u + DIFF_AHEAD] = scores(u + DIFF_AHEAD)

    pl.when(ki < qi)(lambda: update(False))
    pl.when(ki == qi)(lambda: update(True))

    @pl.when(ki == qi)
    def _():
        lam = (jnp.exp(jnp.sum(lq1[...] * lk1[...], axis=-1, keepdims=True))
               - jnp.exp(jnp.sum(lq2[...] * lk2[...], axis=-1, keepdims=True)) + lambda_init)
        o_t = a1[...] / l1[...] - lam * (a2[...] / l2[...])
        o = _rms(o_t.T) * sg_ref[...] * (1.0 - lambda_init)
        o_ref[...] = o.astype(o_ref.dtype)


def _diff_attention(proj, lq1, lk1, lq2, lk2, subln, lambda_init, batch, seq):
    n = batch * seq
    t = DIFF_T
    p4 = proj.reshape(proj.shape[0], batch, seq, LANES)
    qi_tab, ki_tab = _diff_steps(seq)
    qa, kb = _diff_alibi_operands()
    slopes = np.repeat((_alibi_slopes(DIFF_HEADS) * np.float32(LOG2E))[:, None, None], t, axis=2)
    vec = lambda a: a.reshape(1, -1).astype(F32)
    small = lambda a: pl.BlockSpec(a.shape, lambda b, hh, s, qt, kt: (0, 0))
    lqs = [vec(lq1), vec(lk1), vec(lq2), vec(lk2)]
    sg = vec(subln)
    grid_spec = pltpu.PrefetchScalarGridSpec(
        num_scalar_prefetch=2,
        grid=(batch, DIFF_HEADS, len(qi_tab)),
        in_specs=[
            pl.BlockSpec((None, None, t, LANES), lambda b, hh, s, qt, kt: (hh, b, qt[s], 0)),
            pl.BlockSpec((None, None, t, LANES), lambda b, hh, s, qt, kt: (DIFF_HEADS + hh, b, kt[s], 0)),
            pl.BlockSpec((None, None, t, LANES),
                         lambda b, hh, s, qt, kt: (2 * DIFF_HEADS + hh, b, kt[s], 0)),
            pl.BlockSpec((None, t, LANES), lambda b, hh, s, qt, kt: (hh, 0, 0)),
            pl.BlockSpec((None, t, LANES), lambda b, hh, s, qt, kt: (hh, 0, 0)),
            pl.BlockSpec((None, 1, t), lambda b, hh, s, qt, kt: (hh, 0, 0)),
            small(lqs[0]), small(lqs[1]), small(lqs[2]), small(lqs[3]), small(sg),
        ],
        out_specs=pl.BlockSpec((None, None, t, LANES), lambda b, hh, s, qt, kt: (hh, b, qt[s], 0)),
        scratch_shapes=[
            pltpu.VMEM((1, t), F32), pltpu.VMEM((1, t), F32), pltpu.VMEM((2 * DIFF_HEAD_DIM, t), F32),
            pltpu.VMEM((1, t), F32), pltpu.VMEM((1, t), F32), pltpu.VMEM((2 * DIFF_HEAD_DIM, t), F32),
        ],
    )
    o = pl.pallas_call(
        functools.partial(_diff_kernel, lambda_init=lambda_init),
        grid_spec=grid_spec,
        out_shape=jax.ShapeDtypeStruct((DIFF_HEADS, batch, seq, LANES), BF16),
        compiler_params=_params(("parallel", "parallel", "arbitrary")),
        name="diff_attention",
    )(jnp.asarray(qi_tab), jnp.asarray(ki_tab), p4, p4, p4, jnp.asarray(qa), jnp.asarray(kb),
      jnp.asarray(slopes), *lqs, sg)
    return o.reshape(DIFF_HEADS, n, LANES)


def _ffn_ple_kernel(*refs, has_mixer_out):
    if has_mixer_out:
        o_ref, wm_ref, *refs = refs
    x_ref, p_ref, g_ref, wi_ref, wo_ref, gg_ref, wg_ref, wp_ref, pg_ref, out_ref = refs
    x = x_ref[...]
    if has_mixer_out:
        o = jnp.concatenate([o_ref[s] for s in range(o_ref.shape[0])], axis=-1)
        x = x + jnp.dot(o, wm_ref[...], preferred_element_type=F32)
    xn = (_rms(x) * g_ref[...]).astype(BF16)
    e = jnp.dot(p_ref[...].astype(BF16), wp_ref[...], preferred_element_type=F32)
    e = _rms(e) * pg_ref[...]
    h = x
    for c0 in range(0, FFN_HIDDEN, FFN_CHUNK):
        c1 = min(c0 + FFN_CHUNK, FFN_HIDDEN)
        a = jnp.dot(xn, wi_ref[:, c0:c1], preferred_element_type=F32)
        b = jnp.dot(xn, wi_ref[:, FFN_HIDDEN + c0:FFN_HIDDEN + c1], preferred_element_type=F32)
        act = (a * _sigmoid(a) * b).astype(BF16)
        h = h + jnp.dot(act, wo_ref[c0:c1, :], preferred_element_type=F32)
    hn = (_rms(h) * gg_ref[...]).astype(BF16)
    gate = _sigmoid(jnp.dot(hn, wg_ref[...], preferred_element_type=F32))
    out_ref[...] = h + gate * e


def _ffn_ple(h, mixer_out, p, layer, ffn_gain, w_in, w_out, gate_gain, w_gate, w_proj, ple_gain):
    n, d = h.shape
    tm = FFN_TM
    row = lambda i: (i, 0)
    fixed = lambda i: (0, 0)
    resident = lambda w, l: _layer_spec(l, w.shape[1:], fixed, pipeline_mode=pl.Buffered(1))
    operands, specs = [], []
    if mixer_out is not None:
        o, w_mix, mix_layer = mixer_out
        operands += [o, w_mix]
        specs += [pl.BlockSpec((o.shape[0], tm, LANES), lambda i: (0, i, 0)), resident(w_mix, mix_layer)]
    operands += [h, p, ffn_gain.reshape(1, d), w_in, w_out, gate_gain.reshape(1, d), w_gate, w_proj,
                 ple_gain.reshape(1, d)]
    specs += [pl.BlockSpec((tm, d), row), _layer_spec(layer, (tm, p.shape[2]), row),
              pl.BlockSpec((1, d), fixed), resident(w_in, layer), resident(w_out, layer),
              pl.BlockSpec((1, d), fixed), resident(w_gate, layer), resident(w_proj, layer),
              pl.BlockSpec((1, d), fixed)]
    return pl.pallas_call(
        functools.partial(_ffn_ple_kernel, has_mixer_out=mixer_out is not None),
        grid=(n // tm,),
        in_specs=specs,
        out_specs=pl.BlockSpec((tm, d), row),
        out_shape=jax.ShapeDtypeStruct((n, d), F32),
        compiler_params=_params(("parallel",)),
        name="ffn_ple",
    )(*operands)


def _diff_lambda_init(layer_idx):
    return 0.8 - 0.6 * math.exp(-0.3 * layer_idx)


def _slab_gains(groups):
    rows = [jnp.tile(g, LANES // g.shape[0]) for g, count in groups for _ in range(count)]
    return jnp.stack(rows)[:, None, :].astype(F32)


def kernel(x, p, mix_norm, ffn_norm, a_w_in, a_w_out, b_w_in, b_q_norm, b_k_norm, b_w_out,
           c_w_in, c_q_norm, c_k_norm, c_lambda_q1, c_lambda_k1, c_lambda_q2, c_lambda_k2,
           c_subln, c_w_out, ffn_w_in, ffn_w_out, ple_w_proj, ple_norm, ple_gate_norm, ple_w_gate):
    batch, seq, d = x.shape
    depth = p.shape[0]
    n = batch * seq
    h = x.reshape(n, d)
    bf = lambda w: w.astype(BF16)
    a_w_in, a_w_out, b_w_in, b_w_out, c_w_in, c_w_out = map(bf, (a_w_in, a_w_out, b_w_in, b_w_out,
                                                                 c_w_in, c_w_out))
    ffn_w_in, ffn_w_out, ple_w_gate, ple_w_proj = map(bf, (ffn_w_in, ffn_w_out, ple_w_gate, ple_w_proj))
    p3 = p.reshape(depth, n, PLE_DIM)
    ones = jnp.ones((LANES,), F32)
    for i in range(depth):
        kind, j = i % N_MIXERS, i // N_MIXERS
        mixer_out = None
        if kind == 0:
            proj = _norm_proj(h, mix_norm[i], a_w_in, j, PROJ_TN_A)
            h = _retention(proj, h, a_w_out, j, batch, seq)
        elif kind == 1:
            q_gain = b_q_norm[j] * (DIL_HEAD_DIM ** -0.5 * LOG2E)
            gains = _slab_gains([(q_gain, DIL_QK // LANES), (b_k_norm[j], DIL_QK // LANES),
                                 (ones, DIL_HEADS)])
            proj = _norm_proj(h, mix_norm[i], b_w_in, j, PROJ_TN_B, gains, "full", 2 * DIL_QK // LANES,
                              out_dtype=jnp.uint32)
            mixer_out = (_dilated_attention(proj, batch, seq), b_w_out, j)
        else:
            gains = _slab_gains([(c_q_norm[j] * (DIFF_HEAD_DIM ** -0.5 * LOG2E), DIFF_QK // LANES),
                                 (c_k_norm[j], DIFF_QK // LANES), (ones, DIFF_V // LANES)])
            proj = _norm_proj(h, mix_norm[i], c_w_in, j, PROJ_TN_C, gains, "half", 2 * DIFF_QK // LANES)
            o = _diff_attention(proj, c_lambda_q1[j], c_lambda_k1[j], c_lambda_q2[j],
                                c_lambda_k2[j], c_subln[j], _diff_lambda_init(i), batch, seq)
            mixer_out = (o, c_w_out, j)
        h = _ffn_ple(h, mixer_out, p3, i, ffn_norm[i], ffn_w_in, ffn_w_out,
                     ple_gate_norm[i], ple_w_gate, ple_w_proj, ple_norm[i])
    return h.reshape(batch, seq, d)
```

```python
import functools
import math

import numpy as np
import jax
import jax.numpy as jnp
from jax import lax
from jax.experimental import pallas as pl
from jax.experimental.pallas import tpu as pltpu

D_MODEL = 1024
PLE_DIM = 256
N_MIXERS = 3
RMS_EPS = 1e-6
NEG = -1e30

RET_HEADS = 4
RET_DK = 256
RET_DV = 512
RET_CHUNK = 128
RET_IN = 2 * RET_HEADS * RET_DK + 2 * RET_HEADS * RET_DV

DIL_GROUPS = ((128, 1), (512, 4), (2048, 16))
DIL_HEADS = 8
DIL_HEAD_DIM = 128
DIL_BLOCK = 128
DIL_QK = len(DIL_GROUPS) * DIL_HEADS * DIL_HEAD_DIM
DIL_IN = 2 * DIL_QK + DIL_HEADS * DIL_HEAD_DIM

DIFF_HEADS = 8
DIFF_HEAD_DIM = 64
DIFF_QK = 2 * DIFF_HEADS * DIFF_HEAD_DIM
DIFF_V = DIFF_HEADS * 2 * DIFF_HEAD_DIM
DIFF_IN = 2 * DIFF_QK + DIFF_V

FFN_HIDDEN = 2816

LANES = 128
V7X_VMEM_BYTES = 64 * 1024 * 1024
VMEM_LIMIT_BYTES = V7X_VMEM_BYTES - 8 * 1024 * 1024

PROJ_TM = 1024
PROJ_TN_A = 3072
PROJ_TN_B = 3584
PROJ_TN_C = 3072
PROJ_CHUNK = 256
PROJ_AHEAD = 2
FFN_TM = 512
FFN_CHUNK = 1024
RET_BLOCK = 256
RET_BATCH = 2
DIL_WINDOW = 2048
DIL_HOP = 4
DIFF_T = 2048
DIFF_STRIP = 256
DIFF_AHEAD = 8

LOG2E = math.log2(math.e)

F32 = jnp.float32
BF16 = jnp.bfloat16

_NT = (((1,), (1,)), ((), ()))
_TN = (((0,), (0,)), ((), ()))


def _params(semantics):
    return pltpu.CompilerParams(dimension_semantics=semantics, vmem_limit_bytes=VMEM_LIMIT_BYTES)


def _layer_spec(layer, block, index_map, **kwargs):
    return pl.BlockSpec((None,) + tuple(block), lambda *g: (layer,) + tuple(index_map(*g)), **kwargs)


def _rms(x):
    return x * lax.rsqrt(jnp.mean(x * x, axis=-1, keepdims=True) + RMS_EPS)


def _sigmoid(x):
    return 1.0 / (1.0 + jnp.exp(-x))


def _norm_proj_kernel(x_ref, g_ref, w_ref, hg_ref, o_ref, *, head_norm, n_norm_slabs):
    j = pl.program_id(1)
    x = x_ref[...]
    inv = lax.rsqrt(jnp.mean(x * x, axis=-1, keepdims=True) + RMS_EPS)
    xg = (x * g_ref[...]).astype(BF16)

    def full_norm(z, gain):
        return _rms(z) * gain

    def half_norm(z, gain):
        lo = lax.broadcasted_iota(jnp.int32, z.shape, 1) < (LANES // 2)
        zz = z * z
        ss_lo = jnp.sum(jnp.where(lo, zz, 0.0), axis=-1, keepdims=True)
        ss_hi = jnp.sum(jnp.where(lo, 0.0, zz), axis=-1, keepdims=True)
        inv = jnp.where(lo, lax.rsqrt(ss_lo / (LANES // 2) + RMS_EPS),
                        lax.rsqrt(ss_hi / (LANES // 2) + RMS_EPS))
        return z * inv * gain

    norm = {None: None, "full": full_norm, "half": half_norm}[head_norm]
    tile_slabs = o_ref.shape[0]

    def emit(c, y):
        for s in range(PROJ_CHUNK // LANES):
            slab = c * (PROJ_CHUNK // LANES) + s
            z = y[:, s * LANES:(s + 1) * LANES] * inv
            if norm is not None:
                z = jnp.where(j * tile_slabs + slab < n_norm_slabs, norm(z, hg_ref[slab]), z)
            z = z.astype(BF16)
            o_ref[slab] = z if o_ref.dtype == BF16 else pltpu.bitcast(z, o_ref.dtype)

    n_chunks = w_ref.shape[1] // PROJ_CHUNK
    chunk = lambda c: jnp.dot(xg, w_ref[:, c * PROJ_CHUNK:(c + 1) * PROJ_CHUNK], preferred_element_type=F32)
    pending = {c: chunk(c) for c in range(min(PROJ_AHEAD, n_chunks))}
    for c in range(n_chunks):
        y = pending.pop(c)
        if c + PROJ_AHEAD < n_chunks:
            pending[c + PROJ_AHEAD] = chunk(c + PROJ_AHEAD)
        emit(c, y)


def _norm_proj(h, gain, w, layer, tn, slab_gains=None, head_norm=None, n_norm_slabs=0, out_dtype=BF16):
    n, d = h.shape
    n_out = w.shape[2]
    slabs = tn // LANES
    if slab_gains is None:
        slab_gains = jnp.ones((n_out // LANES, 1, LANES), F32)
    pack = 1 if out_dtype == BF16 else 2
    kern = functools.partial(_norm_proj_kernel, head_norm=head_norm, n_norm_slabs=n_norm_slabs)
    return pl.pallas_call(
        kern,
        grid=(n // PROJ_TM, n_out // tn),
        in_specs=[
            pl.BlockSpec((PROJ_TM, d), lambda i, j: (i, 0)),
            pl.BlockSpec((1, d), lambda i, j: (0, 0)),
            _layer_spec(layer, (d, tn), lambda i, j: (0, j)),
            pl.BlockSpec((slabs, 1, LANES), lambda i, j: (j, 0, 0)),
        ],
        out_specs=pl.BlockSpec((slabs, PROJ_TM // pack, LANES), lambda i, j: (j, i, 0)),
        out_shape=jax.ShapeDtypeStruct((n_out // LANES, n // pack, LANES), out_dtype),
        compiler_params=_params(("parallel", "arbitrary")),
        name="norm_proj",
    )(h, gain.reshape(1, d), w, slab_gains)


def _retention_tables():
    h = np.arange(RET_HEADS, dtype=np.float32)
    log_g = np.log(np.float32(1.0) - np.float32(2.0) ** (np.float32(-5.0) - h)).astype(np.float32)
    pos = np.arange(RET_BLOCK, dtype=np.float32)
    rel = pos[:, None] - pos[None, :]
    scale = np.float32(RET_DK ** -0.5)
    din = np.where(rel >= 0, np.exp(np.maximum(rel, 0.0)[None] * log_g[:, None, None]), 0.0)
    dq = np.exp((pos + 1.0)[None] * log_g[:, None])
    dk = np.exp((RET_BLOCK - 1.0 - pos)[None] * log_g[:, None])
    dchunk = np.exp(RET_BLOCK * log_g)
    return ((din * scale).astype(np.float32), dq.astype(np.float32)[:, :, None],
            (dk * scale).astype(np.float32)[:, :, None], tuple(float(v) for v in dchunk))


def _retention_kernel(q_ref, k_ref, v_ref, g_ref, h_ref, wo_ref, din_ref, dq_ref, dk_ref,
                      o_ref, r_ref, *, dchunk):
    @pl.when(pl.program_id(1) == 0)
    def _():
        r_ref[...] = jnp.zeros_like(r_ref)

    qs, vs = RET_DK // LANES, RET_DV // LANES
    chains = [(bb, hh) for bb in range(RET_BATCH) for hh in range(RET_HEADS)]

    def cat(ref, bb, first, count):
        return jnp.concatenate([ref[first + s, bb] for s in range(count)], axis=-1)

    qc = [cat(q_ref, bb, hh * qs, qs) for bb, hh in chains]
    kc = [cat(k_ref, bb, hh * qs, qs) for bb, hh in chains]
    vc = [cat(v_ref, bb, hh * vs, vs) for bb, hh in chains]
    att = [(lax.dot_general(qc[i], kc[i], _NT, preferred_element_type=F32) * din_ref[hh]).astype(BF16)
           for i, (bb, hh) in enumerate(chains)]
    state = [r_ref[bb, hh] for bb, hh in chains]
    y = [jnp.dot(att[i], vc[i], preferred_element_type=F32)
         + jnp.dot(qc[i], state[i].astype(BF16), preferred_element_type=F32) * dq_ref[hh]
         for i, (bb, hh) in enumerate(chains)]
    for i, (bb, hh) in enumerate(chains):
        kd = (kc[i].astype(F32) * dk_ref[hh]).astype(BF16)
        r_ref[bb, hh] = state[i] * dchunk[hh] + lax.dot_general(kd, vc[i], _TN,
                                                                preferred_element_type=F32)
    out = [h_ref[bb] for bb in range(RET_BATCH)]
    for i, (bb, hh) in enumerate(chains):
        gc = cat(g_ref, bb, hh * vs, vs).astype(F32)
        gated = (gc * _sigmoid(gc) * _rms(y[i])).astype(BF16)
        out[bb] = out[bb] + jnp.dot(gated, wo_ref[hh * RET_DV:(hh + 1) * RET_DV, :],
                                    preferred_element_type=F32)
    for bb in range(RET_BATCH):
        o_ref[bb] = out[bb]


def _retention(proj, h, w_out, layer, batch, seq):
    n, d = h.shape
    din, dq, dk, dchunk = _retention_tables()
    p4 = proj.reshape(proj.shape[0], batch, seq, LANES)
    h3 = h.reshape(batch, seq, d)
    nq = RET_HEADS * RET_DK // LANES
    nv = RET_HEADS * RET_DV // LANES
    t, nb = RET_BLOCK, RET_BATCH
    slab_spec = lambda cnt, blk: pl.BlockSpec((cnt, nb, t, LANES), lambda b, i: (blk, b, i, 0))
    const3 = lambda a: pl.BlockSpec(a.shape, lambda b, i: (0, 0, 0))
    out = pl.pallas_call(
        functools.partial(_retention_kernel, dchunk=dchunk),
        grid=(batch // nb, seq // t),
        in_specs=[
            slab_spec(nq, 0), slab_spec(nq, 1), slab_spec(nv, 1), slab_spec(nv, 2),
            pl.BlockSpec((nb, t, d), lambda b, i: (b, i, 0)),
            _layer_spec(layer, w_out.shape[1:], lambda b, i: (0, 0)),
            const3(din), const3(dq), const3(dk),
        ],
        out_specs=pl.BlockSpec((nb, t, d), lambda b, i: (b, i, 0)),
        out_shape=jax.ShapeDtypeStruct((batch, seq, d), F32),
        scratch_shapes=[pltpu.VMEM((nb, RET_HEADS, RET_DK, RET_DV), F32)],
        compiler_params=_params(("parallel", "arbitrary")),
        name="retention",
    )(p4, p4, p4, p4, h3, w_out, jnp.asarray(din), jnp.asarray(dq), jnp.asarray(dk))
    return out.reshape(n, d)


def _alibi_slopes(n):
    ratio = 2.0 ** (-8.0 / n)
    return np.array([ratio ** (i + 1) for i in range(n)], dtype=np.float32)


def _dilated_tables():
    slopes = _alibi_slopes(DIL_HEADS)
    rel = (DIL_BLOCK + np.arange(DIL_BLOCK))[:, None] - np.arange(2 * DIL_BLOCK)[None, :]
    tabs = []
    for window, dilation in DIL_GROUPS:
        valid = (rel >= 0) & (rel <= window // dilation)
        bias = -(slopes[:, None, None] * (rel * dilation).astype(np.float32)[None]) * np.float32(LOG2E)
        tabs.append(np.where(valid[None], bias, np.float32(NEG)).astype(np.float32))
    return np.stack(tabs, axis=1)


def _dilated_kernel(q0, q1, q2, k0, k1, k2, kh0, kh1, kh2, v_ref, vh_ref, tab_ref, o_ref,
                    og_ref, lg_ref, stage_ref):
    first_window = pl.program_id(2) == 0
    in_prev_block = lax.broadcasted_iota(jnp.int32, (DIL_BLOCK, 2 * DIL_BLOCK), 1) < DIL_BLOCK
    ones = jnp.ones((2 * DIL_BLOCK, LANES), BF16)

    def scores(qb, keys, tab):
        return lax.dot_general(qb, keys, _NT, preferred_element_type=F32) + tab

    def attend(s, vals):
        m = jnp.max(s, axis=-1, keepdims=True)
        p = jnp.exp2(s - m).astype(BF16)
        res = jnp.dot(p, jnp.concatenate([vals, ones], axis=1), preferred_element_type=F32)
        den = res[:, LANES:]
        return res[:, :LANES] / den, m + jnp.log2(den)

    for g, (q_ref, k_ref, kh_ref) in enumerate(((q0, k0, kh0), (q1, k1, kh1), (q2, k2, kh2))):
        dilation = DIL_GROUPS[g][1]
        span = DIL_BLOCK * dilation
        n_blocks = DIL_WINDOW // span
        tab = tab_ref[g]
        tab_first = jnp.where(jnp.logical_and(first_window, in_prev_block), NEG, tab)
        if dilation == 1:
            streams = [lambda ref, start: pltpu.bitcast(ref[pl.ds(start // 2, DIL_BLOCK // 2), :], BF16)]
        else:
            def pair(ref, start, half):
                words = ref[pl.ds(start // 2, DIL_BLOCK, stride=dilation // 2), :]
                bits = (words << 16) if half == 0 else (words & jnp.uint32(0xFFFF0000))
                return pltpu.bitcast(bits, F32).astype(BF16)
            streams = [functools.partial(pair, half=0), functools.partial(pair, half=1)]
        two_hops = dilation > DIL_HOP
        part = DIL_WINDOW // DIL_HOP

        def out_rows(r, blk):
            if dilation == 1:
                return pl.ds(blk * span, DIL_BLOCK)
            if two_hops:
                return pl.ds((r % DIL_HOP) * part + r // DIL_HOP, DIL_BLOCK, stride=DIL_HOP)
            return pl.ds(r + blk * span, DIL_BLOCK, stride=dilation)

        jobs = []
        for r0 in range(0, dilation, len(streams)):
            for half, load in enumerate(streams):
                kb = [load(kh_ref, r0)] + [load(k_ref, r0 + blk * span) for blk in range(n_blocks)]
                vb = [load(vh_ref, DIL_WINDOW - span + r0)] + [load(v_ref, r0 + blk * span)
                                                                for blk in range(n_blocks)]
                for blk in range(n_blocks):
                    s = scores(load(q_ref, r0 + blk * span), jnp.concatenate([kb[blk], kb[blk + 1]], axis=0),
                               tab_first if blk == 0 else tab)
                    jobs.append((s, jnp.concatenate([vb[blk], vb[blk + 1]], axis=0),
                                 out_rows(r0 + half, blk)))
        for s, vals, rows in jobs:
            o, lse = attend(s, vals)
            if two_hops:
                stage_ref[0, rows, :] = o
                stage_ref[1, rows, :] = lse
            else:
                og_ref[g, rows, :] = o
                lg_ref[g, rows, :] = lse
        if two_hops:
            assert n_blocks == 1 and dilation == DIL_HOP * DIL_HOP
            for c in range(DIL_HOP):
                og_ref[g, pl.ds(c, part, stride=DIL_HOP), :] = stage_ref[0, c * part:(c + 1) * part, :]
                lg_ref[g, pl.ds(c, part, stride=DIL_HOP), :] = stage_ref[1, c * part:(c + 1) * part, :]
    l0, l1, l2 = lg_ref[0], lg_ref[1], lg_ref[2]
    m = jnp.maximum(jnp.maximum(l0, l1), l2)
    e0, e1, e2 = jnp.exp2(l0 - m), jnp.exp2(l1 - m), jnp.exp2(l2 - m)
    mix = e0 * og_ref[0] + e1 * og_ref[1] + e2 * og_ref[2]
    o_ref[...] = (mix / (e0 + e1 + e2)).astype(o_ref.dtype)


def _dilated_attention(proj, batch, seq):
    n = batch * seq
    p4 = proj.reshape(proj.shape[0], batch, seq // 2, LANES)
    n_groups = len(DIL_GROUPS)
    v0 = 2 * n_groups * DIL_HEADS

    def cur(first):
        return pl.BlockSpec((None, None, DIL_WINDOW // 2, LANES), lambda b, hh, w: (first + hh, b, w, 0))

    def halo(first, rows):
        per_window = DIL_WINDOW // rows
        return pl.BlockSpec((None, None, rows // 2, LANES),
                            lambda b, hh, w: (first + hh, b, jnp.maximum(w * per_window - 1, 0), 0))

    q_specs = [cur(g * DIL_HEADS) for g in range(n_groups)]
    k_specs = [cur((n_groups + g) * DIL_HEADS) for g in range(n_groups)]
    kh_specs = [halo((n_groups + g) * DIL_HEADS, DIL_BLOCK * DIL_GROUPS[g][1]) for g in range(n_groups)]
    tabs = _dilated_tables()
    tab_spec = pl.BlockSpec((None,) + tabs.shape[1:], lambda b, hh, w: (hh, 0, 0, 0))
    o = pl.pallas_call(
        _dilated_kernel,
        grid=(batch, DIL_HEADS, seq // DIL_WINDOW),
        in_specs=q_specs + k_specs + kh_specs + [cur(v0), halo(v0, DIL_WINDOW), tab_spec],
        out_specs=pl.BlockSpec((None, None, DIL_WINDOW, LANES), lambda b, hh, w: (hh, b, w, 0)),
        out_shape=jax.ShapeDtypeStruct((DIL_HEADS, batch, seq, LANES), BF16),
        scratch_shapes=[pltpu.VMEM((n_groups, DIL_WINDOW, LANES), F32),
                        pltpu.VMEM((n_groups, DIL_WINDOW, LANES), F32),
                        pltpu.VMEM((2, DIL_WINDOW, LANES), F32)],
        compiler_params=_params(("parallel", "parallel", "arbitrary")),
        name="dilated_attention",
    )(*([p4] * 11), jnp.asarray(tabs))
    return o.reshape(DIL_HEADS, n, LANES)


def _diff_steps(seq):
    qi, ki = [], []
    for a in range(seq // DIFF_T):
        for b in range(a + 1):
            qi.append(a)
            ki.append(b)
    return np.asarray(qi, np.int32), np.asarray(ki, np.int32)


def _diff_alibi_operands():
    import ml_dtypes
    bf16 = ml_dtypes.bfloat16
    rem = (_alibi_slopes(DIFF_HEADS) * np.float32(LOG2E)).astype(np.float32)
    pieces = []
    for _ in range(3):
        piece = rem.astype(bf16).astype(np.float32)
        pieces.append(piece)
        rem = (rem - piece).astype(np.float32)
    if np.any(rem != 0):
        raise ValueError("slope * log2(e) does not split into three bf16 pieces")
    idx = np.arange(DIFF_T)
    parts = [(idx % 256).astype(np.float32), (idx - idx % 256).astype(np.float32)]
    qa = np.zeros((DIFF_HEADS, DIFF_T, LANES), np.float32)
    kb = np.zeros((DIFF_HEADS, DIFF_T, LANES), np.float32)
    col = 0
    for piece in pieces:
        for part in parts:
            qa[:, :, col] = part[None, :]
            kb[:, :, col] = -piece[:, None]
            qa[:, :, col + 1] = piece[:, None]
            kb[:, :, col + 1] = part[None, :]
            col += 2
    return qa.astype(bf16), kb.astype(bf16)


def _diff_kernel(qi_tab, ki_tab, q_ref, k_ref, v_ref, qa_ref, kb_ref, slope_ref, lq1, lk1, lq2, lk2,
                 sg_ref, o_ref, m1, l1, a1, m2, l2, a2, *, lambda_init):
    step = pl.program_id(2)
    qi = qi_tab[step]
    ki = ki_tab[step]
    slope = slope_ref[...]

    @pl.when(ki == 0)
    def _():
        for m, l, a in ((m1, l1, a1), (m2, l2, a2)):
            m[...] = jnp.full_like(m, NEG)
            l[...] = jnp.zeros_like(l)
            a[...] = jnp.zeros_like(a)

    offset = slope * ((qi - ki) * DIFF_T).astype(F32)

    q = q_ref[...]
    lo = lax.broadcasted_iota(jnp.int32, q.shape, 1) < DIFF_HEAD_DIM
    zero = jnp.zeros_like(q)
    qa = qa_ref[...]
    keys = jnp.concatenate([k_ref[...], kb_ref[...]], axis=1)
    v = v_ref[...]

    n_strips = DIFF_T // DIFF_STRIP
    units = [(s, c) for c in range(n_strips) for s in range(2)]
    stats = ((m1, l1, a1), (m2, l2, a2))

    def update(on_diagonal):
        queries = [jnp.concatenate([jnp.where(lo, q, zero), qa], axis=1),
                   jnp.concatenate([jnp.where(lo, zero, q), qa], axis=1)]

        def n_keys(c):
            return (c + 1) * DIFF_STRIP if on_diagonal else DIFF_T

        def scores(u):
            s, c = units[u]
            return lax.dot_general(keys[:n_keys(c)], queries[s][c * DIFF_STRIP:(c + 1) * DIFF_STRIP],
                                   _NT, preferred_element_type=F32)

        pending = {u: scores(u) for u in range(DIFF_AHEAD)}
        for u, (s, c) in enumerate(units):
            m_ref, l_ref, a_ref = stats[s]
            cols = slice(c * DIFF_STRIP, (c + 1) * DIFF_STRIP)
            t = pending.pop(u)
            if on_diagonal:
                key = lax.broadcasted_iota(jnp.int32, t.shape, 0)
                qry = lax.broadcasted_iota(jnp.int32, t.shape, 1) + c * DIFF_STRIP
                t = jnp.where(key <= qry, t, NEG)
            m_old = m_ref[:, cols]
            m_new = jnp.maximum(m_old, jnp.max(t, axis=0, keepdims=True) - offset[:, cols])
            alpha = jnp.exp2(m_old - m_new)
            p = jnp.exp2(t - (m_new + offset[:, cols]))
            l_ref[:, cols] = alpha * l_ref[:, cols] + jnp.sum(p, axis=0, keepdims=True)
            a_ref[:, cols] = a_ref[:, cols] * alpha + lax.dot_general(
                v[:n_keys(c)], p.astype(BF16), _TN, preferred_element_type=F32)
            m_ref[:, cols] = m_new
            if u + DIFF_AHEAD < len(units):
                pending[u + DIFF_AHEAD] = scores(u + DIFF_AHEAD)

    pl.when(ki < qi)(lambda: update(False))
    pl.when(ki == qi)(lambda: update(True))

    @pl.when(ki == qi)
    def _():
        lam = (jnp.exp(jnp.sum(lq1[...] * lk1[...], axis=-1, keepdims=True))
               - jnp.exp(jnp.sum(lq2[...] * lk2[...], axis=-1, keepdims=True)) + lambda_init)
        o_t = a1[...] / l1[...] - lam * (a2[...] / l2[...])
        o = _rms(o_t.T) * sg_ref[...] * (1.0 - lambda_init)
        o_ref[...] = o.astype(o_ref.dtype)


def _diff_attention(proj, lq1, lk1, lq2, lk2, subln, lambda_init, batch, seq):
    n = batch * seq
    t = DIFF_T
    p4 = proj.reshape(proj.shape[0], batch, seq, LANES)
    qi_tab, ki_tab = _diff_steps(seq)
    qa, kb = _diff_alibi_operands()
    slopes = np.repeat((_alibi_slopes(DIFF_HEADS) * np.float32(LOG2E))[:, None, None], t, axis=2)
    vec = lambda a: a.reshape(1, -1).astype(F32)
    small = lambda a: pl.BlockSpec(a.shape, lambda b, hh, s, qt, kt: (0, 0))
    lqs = [vec(lq1), vec(lk1), vec(lq2), vec(lk2)]
    sg = vec(subln)
    grid_spec = pltpu.PrefetchScalarGridSpec(
        num_scalar_prefetch=2,
        grid=(batch, DIFF_HEADS, len(qi_tab)),
        in_specs=[
            pl.BlockSpec((None, None, t, LANES), lambda b, hh, s, qt, kt: (hh, b, qt[s], 0)),
            pl.BlockSpec((None, None, t, LANES), lambda b, hh, s, qt, kt: (DIFF_HEADS + hh, b, kt[s], 0)),
            pl.BlockSpec((None, None, t, LANES),
                         lambda b, hh, s, qt, kt: (2 * DIFF_HEADS + hh, b, kt[s], 0)),
            pl.BlockSpec((None, t, LANES), lambda b, hh, s, qt, kt: (hh, 0, 0)),
            pl.BlockSpec((None, t, LANES), lambda b, hh, s, qt, kt: (hh, 0, 0)),
            pl.BlockSpec((None, 1, t), lambda b, hh, s, qt, kt: (hh, 0, 0)),
            small(lqs[0]), small(lqs[1]), small(lqs[2]), small(lqs[3]), small(sg),
        ],
        out_specs=pl.BlockSpec((None, None, t, LANES), lambda b, hh, s, qt, kt: (hh, b, qt[s], 0)),
        scratch_shapes=[
            pltpu.VMEM((1, t), F32), pltpu.VMEM((1, t), F32), pltpu.VMEM((2 * DIFF_HEAD_DIM, t), F32),
            pltpu.VMEM((1, t), F32), pltpu.VMEM((1, t), F32), pltpu.VMEM((2 * DIFF_HEAD_DIM, t), F32),
        ],
    )
    o = pl.pallas_call(
        functools.partial(_diff_kernel, lambda_init=lambda_init),
        grid_spec=grid_spec,
        out_shape=jax.ShapeDtypeStruct((DIFF_HEADS, batch, seq, LANES), BF16),
        compiler_params=_params(("parallel", "parallel", "arbitrary")),
        name="diff_attention",
    )(jnp.asarray(qi_tab), jnp.asarray(ki_tab), p4, p4, p4, jnp.asarray(qa), jnp.asarray(kb),
      jnp.asarray(slopes), *lqs, sg)
    return o.reshape(DIFF_HEADS, n, LANES)


def _ffn_ple_kernel(*refs, has_mixer_out):
    if has_mixer_out:
        o_ref, wm_ref, *refs = refs
    x_ref, p_ref, g_ref, wi_ref, wo_ref, gg_ref, wg_ref, wp_ref, pg_ref, out_ref = refs
    x = x_ref[...]
    if has_mixer_out:
        o = jnp.concatenate([o_ref[s] for s in range(o_ref.shape[0])], axis=-1)
        x = x + jnp.dot(o, wm_ref[...], preferred_element_type=F32)
    xn = (_rms(x) * g_ref[...]).astype(BF16)
    e = jnp.dot(p_ref[...].astype(BF16), wp_ref[...], preferred_element_type=F32)
    e = _rms(e) * pg_ref[...]
    h = x
    for c0 in range(0, FFN_HIDDEN, FFN_CHUNK):
        c1 = min(c0 + FFN_CHUNK, FFN_HIDDEN)
        a = jnp.dot(xn, wi_ref[:, c0:c1], preferred_element_type=F32)
        b = jnp.dot(xn, wi_ref[:, FFN_HIDDEN + c0:FFN_HIDDEN + c1], preferred_element_type=F32)
        act = (a * _sigmoid(a) * b).astype(BF16)
        h = h + jnp.dot(act, wo_ref[c0:c1, :], preferred_element_type=F32)
    hn = (_rms(h) * gg_ref[...]).astype(BF16)
    gate = _sigmoid(jnp.dot(hn, wg_ref[...], preferred_element_type=F32))
    out_ref[...] = h + gate * e


def _ffn_ple(h, mixer_out, p, layer, ffn_gain, w_in, w_out, gate_gain, w_gate, w_proj, ple_gain):
    n, d = h.shape
    tm = FFN_TM
    row = lambda i: (i, 0)
    fixed = lambda i: (0, 0)
    resident = lambda w, l: _layer_spec(l, w.shape[1:], fixed, pipeline_mode=pl.Buffered(1))
    operands, specs = [], []
    if mixer_out is not None:
        o, w_mix, mix_layer = mixer_out
        operands += [o, w_mix]
        specs += [pl.BlockSpec((o.shape[0], tm, LANES), lambda i: (0, i, 0)), resident(w_mix, mix_layer)]
    operands += [h, p, ffn_gain.reshape(1, d), w_in, w_out, gate_gain.reshape(1, d), w_gate, w_proj,
                 ple_gain.reshape(1, d)]
    specs += [pl.BlockSpec((tm, d), row), _layer_spec(layer, (tm, p.shape[2]), row),
              pl.BlockSpec((1, d), fixed), resident(w_in, layer), resident(w_out, layer),
              pl.BlockSpec((1, d), fixed), resident(w_gate, layer), resident(w_proj, layer),
              pl.BlockSpec((1, d), fixed)]
    return pl.pallas_call(
        functools.partial(_ffn_ple_kernel, has_mixer_out=mixer_out is not None),
        grid=(n // tm,),
        in_specs=specs,
        out_specs=pl.BlockSpec((tm, d), row),
        out_shape=jax.ShapeDtypeStruct((n, d), F32),
        compiler_params=_params(("parallel",)),
        name="ffn_ple",
    )(*operands)


def _diff_lambda_init(layer_idx):
    return 0.8 - 0.6 * math.exp(-0.3 * layer_idx)


def _slab_gains(groups):
    rows = [jnp.tile(g, LANES // g.shape[0]) for g, count in groups for _ in range(count)]
    return jnp.stack(rows)[:, None, :].astype(F32)


def kernel(x, p, mix_norm, ffn_norm, a_w_in, a_w_out, b_w_in, b_q_norm, b_k_norm, b_w_out,
           c_w_in, c_q_norm, c_k_norm, c_lambda_q1, c_lambda_k1, c_lambda_q2, c_lambda_k2,
           c_subln, c_w_out, ffn_w_in, ffn_w_out, ple_w_proj, ple_norm, ple_gate_norm, ple_w_gate):
    batch, seq, d = x.shape
    depth = p.shape[0]
    n = batch * seq
    h = x.reshape(n, d)
    bf = lambda w: w.astype(BF16)
    a_w_in, a_w_out, b_w_in, b_w_out, c_w_in, c_w_out = map(bf, (a_w_in, a_w_out, b_w_in, b_w_out,
                                                                 c_w_in, c_w_out))
    ffn_w_in, ffn_w_out, ple_w_gate, ple_w_proj = map(bf, (ffn_w_in, ffn_w_out, ple_w_gate, ple_w_proj))
    p3 = p.reshape(depth, n, PLE_DIM)
    ones = jnp.ones((LANES,), F32)
    for i in range(depth):
        kind, j = i % N_MIXERS, i // N_MIXERS
        mixer_out = None
        if kind == 0:
            proj = _norm_proj(h, mix_norm[i], a_w_in, j, PROJ_TN_A)
            h = _retention(proj, h, a_w_out, j, batch, seq)
        elif kind == 1:
            q_gain = b_q_norm[j] * (DIL_HEAD_DIM ** -0.5 * LOG2E)
            gains = _slab_gains([(q_gain, DIL_QK // LANES), (b_k_norm[j], DIL_QK // LANES),
                                 (ones, DIL_HEADS)])
            proj = _norm_proj(h, mix_norm[i], b_w_in, j, PROJ_TN_B, gains, "full", 2 * DIL_QK // LANES,
                              out_dtype=jnp.uint32)
            mixer_out = (_dilated_attention(proj, batch, seq), b_w_out, j)
        else:
            gains = _slab_gains([(c_q_norm[j] * (DIFF_HEAD_DIM ** -0.5 * LOG2E), DIFF_QK // LANES),
                                 (c_k_norm[j], DIFF_QK // LANES), (ones, DIFF_V // LANES)])
            proj = _norm_proj(h, mix_norm[i], c_w_in, j, PROJ_TN_C, gains, "half", 2 * DIFF_QK // LANES)
            o = _diff_attention(proj, c_lambda_q1[j], c_lambda_k1[j], c_lambda_q2[j],
                                c_lambda_k2[j], c_subln[j], _diff_lambda_init(i), batch, seq)
            mixer_out = (o, c_w_out, j)
        h = _ffn_ple(h, mixer_out, p3, i, ffn_norm[i], ffn_w_in, ffn_w_out,
                     ple_gate_norm[i], ple_w_gate, ple_w_proj, ple_norm[i])
    return h.reshape(batch, seq, d)
```

```python
import functools
import math

import numpy as np
import jax
import jax.numpy as jnp
from jax import lax
from jax.experimental import pallas as pl
from jax.experimental.pallas import tpu as pltpu

D_MODEL = 1024
PLE_DIM = 256
N_MIXERS = 3
RMS_EPS = 1e-6
NEG = -1e30

RET_HEADS = 4
RET_DK = 256
RET_DV = 512
RET_CHUNK = 128
RET_IN = 2 * RET_HEADS * RET_DK + 2 * RET_HEADS * RET_DV

DIL_GROUPS = ((128, 1), (512, 4), (2048, 16))
DIL_HEADS = 8
DIL_HEAD_DIM = 128
DIL_BLOCK = 128
DIL_QK = len(DIL_GROUPS) * DIL_HEADS * DIL_HEAD_DIM
DIL_IN = 2 * DIL_QK + DIL_HEADS * DIL_HEAD_DIM

DIFF_HEADS = 8
DIFF_HEAD_DIM = 64
DIFF_QK = 2 * DIFF_HEADS * DIFF_HEAD_DIM
DIFF_V = DIFF_HEADS * 2 * DIFF_HEAD_DIM
DIFF_IN = 2 * DIFF_QK + DIFF_V

FFN_HIDDEN = 2816

LANES = 128
V7X_VMEM_BYTES = 64 * 1024 * 1024
VMEM_LIMIT_BYTES = V7X_VMEM_BYTES - 8 * 1024 * 1024

PROJ_TM = 1024
PROJ_TN_A = 3072
PROJ_TN_B = 3584
PROJ_TN_C = 3072
PROJ_CHUNK = 256
PROJ_AHEAD = 2
FFN_TM = 512
FFN_CHUNK = 1024
RET_BLOCK = 256
RET_BATCH = 2
DIL_WINDOW = 2048
DIL_HOP = 4
DIFF_T = 4096
DIFF_STRIP = 256
DIFF_AHEAD = 8

LOG2E = math.log2(math.e)

F32 = jnp.float32
BF16 = jnp.bfloat16

_NT = (((1,), (1,)), ((), ()))
_TN = (((0,), (0,)), ((), ()))


def _params(semantics):
    return pltpu.CompilerParams(dimension_semantics=semantics, vmem_limit_bytes=VMEM_LIMIT_BYTES)


def _layer_spec(layer, block, index_map, **kwargs):
    return pl.BlockSpec((None,) + tuple(block), lambda *g: (layer,) + tuple(index_map(*g)), **kwargs)


def _rms(x):
    return x * lax.rsqrt(jnp.mean(x * x, axis=-1, keepdims=True) + RMS_EPS)


def _sigmoid(x):
    return 1.0 / (1.0 + jnp.exp(-x))


def _norm_proj_kernel(x_ref, g_ref, w_ref, hg_ref, o_ref, xn_ref, *, head_norm, n_norm_slabs):
    j = pl.program_id(1)

    @pl.when(j == 0)
    def _():
        xn_ref[...] = (_rms(x_ref[...]) * g_ref[...]).astype(BF16)

    def full_norm(z, gain):
        return _rms(z) * gain

    def half_norm(z, gain):
        lo = lax.broadcasted_iota(jnp.int32, z.shape, 1) < (LANES // 2)
        zz = z * z
        ss_lo = jnp.sum(jnp.where(lo, zz, 0.0), axis=-1, keepdims=True)
        ss_hi = jnp.sum(jnp.where(lo, 0.0, zz), axis=-1, keepdims=True)
        inv = jnp.where(lo, lax.rsqrt(ss_lo / (LANES // 2) + RMS_EPS),
                        lax.rsqrt(ss_hi / (LANES // 2) + RMS_EPS))
        return z * inv * gain

    norm = {None: None, "full": full_norm, "half": half_norm}[head_norm]
    tile_slabs = o_ref.shape[0]

    def emit(c, y):
        for s in range(PROJ_CHUNK // LANES):
            slab = c * (PROJ_CHUNK // LANES) + s
            z = y[:, s * LANES:(s + 1) * LANES]
            if norm is not None:
                z = jnp.where(j * tile_slabs + slab < n_norm_slabs, norm(z, hg_ref[slab]), z)
            z = z.astype(BF16)
            o_ref[slab] = z if o_ref.dtype == BF16 else pltpu.bitcast(z, o_ref.dtype)

    xn = xn_ref[...]
    n_chunks = w_ref.shape[1] // PROJ_CHUNK
    chunk = lambda c: jnp.dot(xn, w_ref[:, c * PROJ_CHUNK:(c + 1) * PROJ_CHUNK], preferred_element_type=F32)
    pending = {c: chunk(c) for c in range(min(PROJ_AHEAD, n_chunks))}
    for c in range(n_chunks):
        y = pending.pop(c)
        if c + PROJ_AHEAD < n_chunks:
            pending[c + PROJ_AHEAD] = chunk(c + PROJ_AHEAD)
        emit(c, y)


def _norm_proj(h, gain, w, layer, tn, slab_gains=None, head_norm=None, n_norm_slabs=0, out_dtype=BF16):
    n, d = h.shape
    n_out = w.shape[2]
    slabs = tn // LANES
    if slab_gains is None:
        slab_gains = jnp.ones((n_out // LANES, 1, LANES), F32)
    pack = 1 if out_dtype == BF16 else 2
    kern = functools.partial(_norm_proj_kernel, head_norm=head_norm, n_norm_slabs=n_norm_slabs)
    return pl.pallas_call(
        kern,
        grid=(n // PROJ_TM, n_out // tn),
        in_specs=[
            pl.BlockSpec((PROJ_TM, d), lambda i, j: (i, 0)),
            pl.BlockSpec((1, d), lambda i, j: (0, 0)),
            _layer_spec(layer, (d, tn), lambda i, j: (0, j)),
            pl.BlockSpec((slabs, 1, LANES), lambda i, j: (j, 0, 0)),
        ],
        out_specs=pl.BlockSpec((slabs, PROJ_TM // pack, LANES), lambda i, j: (j, i, 0)),
        out_shape=jax.ShapeDtypeStruct((n_out // LANES, n // pack, LANES), out_dtype),
        scratch_shapes=[pltpu.VMEM((PROJ_TM, d), BF16)],
        compiler_params=_params(("parallel", "arbitrary")),
        name="norm_proj",
    )(h, gain.reshape(1, d), w, slab_gains)


def _retention_tables():
    h = np.arange(RET_HEADS, dtype=np.float32)
    log_g = np.log(np.float32(1.0) - np.float32(2.0) ** (np.float32(-5.0) - h)).astype(np.float32)
    pos = np.arange(RET_BLOCK, dtype=np.float32)
    rel = pos[:, None] - pos[None, :]
    scale = np.float32(RET_DK ** -0.5)
    din = np.where(rel >= 0, np.exp(np.maximum(rel, 0.0)[None] * log_g[:, None, None]), 0.0)
    dq = np.exp((pos + 1.0)[None] * log_g[:, None])
    dk = np.exp((RET_BLOCK - 1.0 - pos)[None] * log_g[:, None])
    dchunk = np.exp(RET_BLOCK * log_g)
    return ((din * scale).astype(np.float32), dq.astype(np.float32)[:, :, None],
            (dk * scale).astype(np.float32)[:, :, None], tuple(float(v) for v in dchunk))


def _retention_kernel(q_ref, k_ref, v_ref, g_ref, h_ref, wo_ref, din_ref, dq_ref, dk_ref,
                      o_ref, r_ref, *, dchunk):
    @pl.when(pl.program_id(1) == 0)
    def _():
        r_ref[...] = jnp.zeros_like(r_ref)

    qs, vs = RET_DK // LANES, RET_DV // LANES
    chains = [(bb, hh) for bb in range(RET_BATCH) for hh in range(RET_HEADS)]

    def cat(ref, bb, first, count):
        return jnp.concatenate([ref[first + s, bb] for s in range(count)], axis=-1)

    qc = [cat(q_ref, bb, hh * qs, qs) for bb, hh in chains]
    kc = [cat(k_ref, bb, hh * qs, qs) for bb, hh in chains]
    vc = [cat(v_ref, bb, hh * vs, vs) for bb, hh in chains]
    att = [(lax.dot_general(qc[i], kc[i], _NT, preferred_element_type=F32) * din_ref[hh]).astype(BF16)
           for i, (bb, hh) in enumerate(chains)]
    state = [r_ref[bb, hh] for bb, hh in chains]
    y = [jnp.dot(att[i], vc[i], preferred_element_type=F32)
         + jnp.dot(qc[i], state[i].astype(BF16), preferred_element_type=F32) * dq_ref[hh]
         for i, (bb, hh) in enumerate(chains)]
    for i, (bb, hh) in enumerate(chains):
        kd = (kc[i].astype(F32) * dk_ref[hh]).astype(BF16)
        r_ref[bb, hh] = state[i] * dchunk[hh] + lax.dot_general(kd, vc[i], _TN,
                                                                preferred_element_type=F32)
    out = [h_ref[bb] for bb in range(RET_BATCH)]
    for i, (bb, hh) in enumerate(chains):
        gc = cat(g_ref, bb, hh * vs, vs).astype(F32)
        gated = (gc * _sigmoid(gc) * _rms(y[i])).astype(BF16)
        out[bb] = out[bb] + jnp.dot(gated, wo_ref[hh * RET_DV:(hh + 1) * RET_DV, :],
                                    preferred_element_type=F32)
    for bb in range(RET_BATCH):
        o_ref[bb] = out[bb]


def _retention(proj, h, w_out, layer, batch, seq):
    n, d = h.shape
    din, dq, dk, dchunk = _retention_tables()
    p4 = proj.reshape(proj.shape[0], batch, seq, LANES)
    h3 = h.reshape(batch, seq, d)
    nq = RET_HEADS * RET_DK // LANES
    nv = RET_HEADS * RET_DV // LANES
    t, nb = RET_BLOCK, RET_BATCH
    slab_spec = lambda cnt, blk: pl.BlockSpec((cnt, nb, t, LANES), lambda b, i: (blk, b, i, 0))
    const3 = lambda a: pl.BlockSpec(a.shape, lambda b, i: (0, 0, 0))
    out = pl.pallas_call(
        functools.partial(_retention_kernel, dchunk=dchunk),
        grid=(batch // nb, seq // t),
        in_specs=[
            slab_spec(nq, 0), slab_spec(nq, 1), slab_spec(nv, 1), slab_spec(nv, 2),
            pl.BlockSpec((nb, t, d), lambda b, i: (b, i, 0)),
            _layer_spec(layer, w_out.shape[1:], lambda b, i: (0, 0)),
            const3(din), const3(dq), const3(dk),
        ],
        out_specs=pl.BlockSpec((nb, t, d), lambda b, i: (b, i, 0)),
        out_shape=jax.ShapeDtypeStruct((batch, seq, d), F32),
        scratch_shapes=[pltpu.VMEM((nb, RET_HEADS, RET_DK, RET_DV), F32)],
        compiler_params=_params(("parallel", "arbitrary")),
        name="retention",
    )(p4, p4, p4, p4, h3, w_out, jnp.asarray(din), jnp.asarray(dq), jnp.asarray(dk))
    return out.reshape(n, d)


def _alibi_slopes(n):
    ratio = 2.0 ** (-8.0 / n)
    return np.array([ratio ** (i + 1) for i in range(n)], dtype=np.float32)


def _dilated_tables():
    slopes = _alibi_slopes(DIL_HEADS)
    rel = (DIL_BLOCK + np.arange(DIL_BLOCK))[:, None] - np.arange(2 * DIL_BLOCK)[None, :]
    tabs = []
    for window, dilation in DIL_GROUPS:
        valid = (rel >= 0) & (rel <= window // dilation)
        bias = -(slopes[:, None, None] * (rel * dilation).astype(np.float32)[None]) * np.float32(LOG2E)
        tabs.append(np.where(valid[None], bias, np.float32(NEG)).astype(np.float32))
    return np.stack(tabs, axis=1)


def _dilated_kernel(q0, q1, q2, k0, k1, k2, kh0, kh1, kh2, v_ref, vh_ref, tab_ref, o_ref,
                    og_ref, lg_ref, stage_ref):
    first_window = pl.program_id(2) == 0
    in_prev_block = lax.broadcasted_iota(jnp.int32, (DIL_BLOCK, 2 * DIL_BLOCK), 1) < DIL_BLOCK
    ones = jnp.ones((2 * DIL_BLOCK, LANES), BF16)

    def attend(qb, keys, vals, tab):
        s = lax.dot_general(qb, keys, _NT, preferred_element_type=F32) + tab
        m = jnp.max(s, axis=-1, keepdims=True)
        p = jnp.exp2(s - m).astype(BF16)
        res = jnp.dot(p, jnp.concatenate([vals, ones], axis=1), preferred_element_type=F32)
        den = res[:, LANES:]
        return res[:, :LANES] / den, m + jnp.log2(den)

    for g, (q_ref, k_ref, kh_ref) in enumerate(((q0, k0, kh0), (q1, k1, kh1), (q2, k2, kh2))):
        dilation = DIL_GROUPS[g][1]
        span = DIL_BLOCK * dilation
        n_blocks = DIL_WINDOW // span
        tab = tab_ref[g]
        tab_first = jnp.where(jnp.logical_and(first_window, in_prev_block), NEG, tab)
        if dilation == 1:
            streams = [lambda ref, start: pltpu.bitcast(ref[pl.ds(start // 2, DIL_BLOCK // 2), :], BF16)]
        else:
            def pair(ref, start, half):
                words = ref[pl.ds(start // 2, DIL_BLOCK, stride=dilation // 2), :]
                bits = (words << 16) if half == 0 else (words & jnp.uint32(0xFFFF0000))
                return pltpu.bitcast(bits, F32).astype(BF16)
            streams = [functools.partial(pair, half=0), functools.partial(pair, half=1)]
        two_hops = dilation > DIL_HOP
        part = DIL_WINDOW // DIL_HOP

        def out_rows(r, blk):
            if dilation == 1:
                return pl.ds(blk * span, DIL_BLOCK)
            if two_hops:
                return pl.ds((r % DIL_HOP) * part + r // DIL_HOP, DIL_BLOCK, stride=DIL_HOP)
            return pl.ds(r + blk * span, DIL_BLOCK, stride=dilation)

        for r0 in range(0, dilation, len(streams)):
            for half, load in enumerate(streams):
                r = r0 + half
                kb = [load(kh_ref, r0)] + [load(k_ref, r0 + blk * span) for blk in range(n_blocks)]
                vb = [load(vh_ref, DIL_WINDOW - span + r0)] + [load(v_ref, r0 + blk * span)
                                                                for blk in range(n_blocks)]
                for blk in range(n_blocks):
                    o, lse = attend(load(q_ref, r0 + blk * span),
                                    jnp.concatenate([kb[blk], kb[blk + 1]], axis=0),
                                    jnp.concatenate([vb[blk], vb[blk + 1]], axis=0),
                                    tab_first if blk == 0 else tab)
                    if two_hops:
                        stage_ref[0, out_rows(r, blk), :] = o
                        stage_ref[1, out_rows(r, blk), :] = lse
                    else:
                        og_ref[g, out_rows(r, blk), :] = o
                        lg_ref[g, out_rows(r, blk), :] = lse
        if two_hops:
            assert n_blocks == 1 and dilation == DIL_HOP * DIL_HOP
            for c in range(DIL_HOP):
                og_ref[g, pl.ds(c, part, stride=DIL_HOP), :] = stage_ref[0, c * part:(c + 1) * part, :]
                lg_ref[g, pl.ds(c, part, stride=DIL_HOP), :] = stage_ref[1, c * part:(c + 1) * part, :]
    l0, l1, l2 = lg_ref[0], lg_ref[1], lg_ref[2]
    m = jnp.maximum(jnp.maximum(l0, l1), l2)
    e0, e1, e2 = jnp.exp2(l0 - m), jnp.exp2(l1 - m), jnp.exp2(l2 - m)
    mix = e0 * og_ref[0] + e1 * og_ref[1] + e2 * og_ref[2]
    o_ref[...] = (mix / (e0 + e1 + e2)).astype(o_ref.dtype)


def _dilated_attention(proj, batch, seq):
    n = batch * seq
    p4 = proj.reshape(proj.shape[0], batch, seq // 2, LANES)
    n_groups = len(DIL_GROUPS)
    v0 = 2 * n_groups * DIL_HEADS

    def cur(first):
        return pl.BlockSpec((None, None, DIL_WINDOW // 2, LANES), lambda b, hh, w: (first + hh, b, w, 0))

    def halo(first, rows):
        per_window = DIL_WINDOW // rows
        return pl.BlockSpec((None, None, rows // 2, LANES),
                            lambda b, hh, w: (first + hh, b, jnp.maximum(w * per_window - 1, 0), 0))

    q_specs = [cur(g * DIL_HEADS) for g in range(n_groups)]
    k_specs = [cur((n_groups + g) * DIL_HEADS) for g in range(n_groups)]
    kh_specs = [halo((n_groups + g) * DIL_HEADS, DIL_BLOCK * DIL_GROUPS[g][1]) for g in range(n_groups)]
    tabs = _dilated_tables()
    tab_spec = pl.BlockSpec((None,) + tabs.shape[1:], lambda b, hh, w: (hh, 0, 0, 0))
    o = pl.pallas_call(
        _dilated_kernel,
        grid=(batch, DIL_HEADS, seq // DIL_WINDOW),
        in_specs=q_specs + k_specs + kh_specs + [cur(v0), halo(v0, DIL_WINDOW), tab_spec],
        out_specs=pl.BlockSpec((None, None, DIL_WINDOW, LANES), lambda b, hh, w: (hh, b, w, 0)),
        out_shape=jax.ShapeDtypeStruct((DIL_HEADS, batch, seq, LANES), BF16),
        scratch_shapes=[pltpu.VMEM((n_groups, DIL_WINDOW, LANES), F32),
                        pltpu.VMEM((n_groups, DIL_WINDOW, LANES), F32),
                        pltpu.VMEM((2, DIL_WINDOW, LANES), F32)],
        compiler_params=_params(("parallel", "parallel", "arbitrary")),
        name="dilated_attention",
    )(*([p4] * 11), jnp.asarray(tabs))
    return o.reshape(DIL_HEADS, n, LANES)


def _diff_steps(seq):
    qi, ki = [], []
    for a in range(seq // DIFF_T):
        for b in range(a + 1):
            qi.append(a)
            ki.append(b)
    return np.asarray(qi, np.int32), np.asarray(ki, np.int32)


def _diff_alibi_operands():
    import ml_dtypes
    bf16 = ml_dtypes.bfloat16
    rem = (_alibi_slopes(DIFF_HEADS) * np.float32(LOG2E)).astype(np.float32)
    pieces = []
    for _ in range(3):
        piece = rem.astype(bf16).astype(np.float32)
        pieces.append(piece)
        rem = (rem - piece).astype(np.float32)
    if np.any(rem != 0):
        raise ValueError("slope * log2(e) does not split into three bf16 pieces")
    idx = np.arange(DIFF_T)
    parts = [(idx % 256).astype(np.float32), (idx - idx % 256).astype(np.float32)]
    qa = np.zeros((DIFF_HEADS, DIFF_T, LANES), np.float32)
    kb = np.zeros((DIFF_HEADS, DIFF_T, LANES), np.float32)
    col = 0
    for piece in pieces:
        for part in parts:
            qa[:, :, col] = part[None, :]
            kb[:, :, col] = -piece[:, None]
            qa[:, :, col + 1] = piece[:, None]
            kb[:, :, col + 1] = part[None, :]
            col += 2
    return qa.astype(bf16), kb.astype(bf16)


def _diff_kernel(qi_tab, ki_tab, q_ref, k_ref, v_ref, qa_ref, kb_ref, slope_ref, lq1, lk1, lq2, lk2,
                 sg_ref, o_ref, m1, l1, a1, m2, l2, a2, *, lambda_init):
    step = pl.program_id(2)
    qi = qi_tab[step]
    ki = ki_tab[step]
    slope = slope_ref[...]

    @pl.when(ki == 0)
    def _():
        for m, l, a in ((m1, l1, a1), (m2, l2, a2)):
            m[...] = jnp.full_like(m, NEG)
            l[...] = jnp.zeros_like(l)
            a[...] = jnp.zeros_like(a)

    offset = slope * ((qi - ki) * DIFF_T).astype(F32)

    q = q_ref[...]
    lo = lax.broadcasted_iota(jnp.int32, q.shape, 1) < DIFF_HEAD_DIM
    zero = jnp.zeros_like(q)
    qa = qa_ref[...]
    keys = jnp.concatenate([k_ref[...], kb_ref[...]], axis=1)
    v = v_ref[...]

    n_strips = DIFF_T // DIFF_STRIP
    units = [(s, c) for c in range(n_strips) for s in range(2)]
    stats = ((m1, l1, a1), (m2, l2, a2))

    def update(on_diagonal):
        queries = [jnp.concatenate([jnp.where(lo, q, zero), qa], axis=1),
                   jnp.concatenate([jnp.where(lo, zero, q), qa], axis=1)]

        def n_keys(c):
            return (c + 1) * DIFF_STRIP if on_diagonal else DIFF_T

        def scores(u):
            s, c = units[u]
            return lax.dot_general(keys[:n_keys(c)], queries[s][c * DIFF_STRIP:(c + 1) * DIFF_STRIP],
                                   _NT, preferred_element_type=F32)

        pending = {u: scores(u) for u in range(DIFF_AHEAD)}
        for u, (s, c) in enumerate(units):
            m_ref, l_ref, a_ref = stats[s]
            cols = slice(c * DIFF_STRIP, (c + 1) * DIFF_STRIP)
            t = pending.pop(u)
            if on_diagonal:
                key = lax.broadcasted_iota(jnp.int32, t.shape, 0)
                qry = lax.broadcasted_iota(jnp.int32, t.shape, 1) + c * DIFF_STRIP
                t = jnp.where(key <= qry, t, NEG)
            m_old = m_ref[:, cols]
            m_new = jnp.maximum(m_old, jnp.max(t, axis=0, keepdims=True) - offset[:, cols])
            alpha = jnp.exp2(m_old - m_new)
            p = jnp.exp2(t - (m_new + offset[:, cols]))
            l_ref[:, cols] = alpha * l_ref[:, cols] + jnp.sum(p, axis=0, keepdims=True)
            a_ref[:, cols] = a_ref[:, cols] * alpha + lax.dot_general(
                v[:n_keys(c)], p.astype(BF16), _TN, preferred_element_type=F32)
            m_ref[:, cols] = m_new
            if u + DIFF_AHEAD < len(units):
                pending[u + DIFF_AHEAD] = scores(u + DIFF_AHEAD)

    pl.when(ki < qi)(lambda: update(False))
    pl.when(ki == qi)(lambda: update(True))

    @pl.when(ki == qi)
    def _():
        lam = (jnp.exp(jnp.sum(lq1[...] * lk1[...], axis=-1, keepdims=True))
               - jnp.exp(jnp.sum(lq2[...] * lk2[...], axis=-1, keepdims=True)) + lambda_init)
        o_t = a1[...] / l1[...] - lam * (a2[...] / l2[...])
        o = _rms(o_t.T) * sg_ref[...] * (1.0 - lambda_init)
        o_ref[...] = o.astype(o_ref.dtype)


def _diff_attention(proj, lq1, lk1, lq2, lk2, subln, lambda_init, batch, seq):
    n = batch * seq
    t = DIFF_T
    p4 = proj.reshape(proj.shape[0], batch, seq, LANES)
    qi_tab, ki_tab = _diff_steps(seq)
    qa, kb = _diff_alibi_operands()
    slopes = np.repeat((_alibi_slopes(DIFF_HEADS) * np.float32(LOG2E))[:, None, None], t, axis=2)
    vec = lambda a: a.reshape(1, -1).astype(F32)
    small = lambda a: pl.BlockSpec(a.shape, lambda b, hh, s, qt, kt: (0, 0))
    lqs = [vec(lq1), vec(lk1), vec(lq2), vec(lk2)]
    sg = vec(subln)
    grid_spec = pltpu.PrefetchScalarGridSpec(
        num_scalar_prefetch=2,
        grid=(batch, DIFF_HEADS, len(qi_tab)),
        in_specs=[
            pl.BlockSpec((None, None, t, LANES), lambda b, hh, s, qt, kt: (hh, b, qt[s], 0)),
            pl.BlockSpec((None, None, t, LANES), lambda b, hh, s, qt, kt: (DIFF_HEADS + hh, b, kt[s], 0)),
            pl.BlockSpec((None, None, t, LANES),
                         lambda b, hh, s, qt, kt: (2 * DIFF_HEADS + hh, b, kt[s], 0)),
            pl.BlockSpec((None, t, LANES), lambda b, hh, s, qt, kt: (hh, 0, 0)),
            pl.BlockSpec((None, t, LANES), lambda b, hh, s, qt, kt: (hh, 0, 0)),
            pl.BlockSpec((None, 1, t), lambda b, hh, s, qt, kt: (hh, 0, 0)),
            small(lqs[0]), small(lqs[1]), small(lqs[2]), small(lqs[3]), small(sg),
        ],
        out_specs=pl.BlockSpec((None, None, t, LANES), lambda b, hh, s, qt, kt: (hh, b, qt[s], 0)),
        scratch_shapes=[
            pltpu.VMEM((1, t), F32), pltpu.VMEM((1, t), F32), pltpu.VMEM((2 * DIFF_HEAD_DIM, t), F32),
            pltpu.VMEM((1, t), F32), pltpu.VMEM((1, t), F32), pltpu.VMEM((2 * DIFF_HEAD_DIM, t), F32),
        ],
    )
    o = pl.pallas_call(
        functools.partial(_diff_kernel, lambda_init=lambda_init),
        grid_spec=grid_spec,
        out_shape=jax.ShapeDtypeStruct((DIFF_HEADS, batch, seq, LANES), BF16),
        compiler_params=_params(("parallel", "parallel", "arbitrary")),
        name="diff_attention",
    )(jnp.asarray(qi_tab), jnp.asarray(ki_tab), p4, p4, p4, jnp.asarray(qa), jnp.asarray(kb),
      jnp.asarray(slopes), *lqs, sg)
    return o.reshape(DIFF_HEADS, n, LANES)


def _ffn_ple_kernel(*refs, has_mixer_out):
    if has_mixer_out:
        o_ref, wm_ref, *refs = refs
    x_ref, p_ref, g_ref, wi_ref, wo_ref, gg_ref, wg_ref, wp_ref, pg_ref, out_ref = refs
    x = x_ref[...]
    if has_mixer_out:
        o = jnp.concatenate([o_ref[s] for s in range(o_ref.shape[0])], axis=-1)
        x = x + jnp.dot(o, wm_ref[...], preferred_element_type=F32)
    xn = (_rms(x) * g_ref[...]).astype(BF16)
    e = jnp.dot(p_ref[...].astype(BF16), wp_ref[...], preferred_element_type=F32)
    e = _rms(e) * pg_ref[...]
    h = x
    for c0 in range(0, FFN_HIDDEN, FFN_CHUNK):
        c1 = min(c0 + FFN_CHUNK, FFN_HIDDEN)
        a = jnp.dot(xn, wi_ref[:, c0:c1], preferred_element_type=F32)
        b = jnp.dot(xn, wi_ref[:, FFN_HIDDEN + c0:FFN_HIDDEN + c1], preferred_element_type=F32)
        act = (a * _sigmoid(a) * b).astype(BF16)
        h = h + jnp.dot(act, wo_ref[c0:c1, :], preferred_element_type=F32)
    hn = (_rms(h) * gg_ref[...]).astype(BF16)
    gate = _sigmoid(jnp.dot(hn, wg_ref[...], preferred_element_type=F32))
    out_ref[...] = h + gate * e


def _ffn_ple(h, mixer_out, p, layer, ffn_gain, w_in, w_out, gate_gain, w_gate, w_proj, ple_gain):
    n, d = h.shape
    tm = FFN_TM
    row = lambda i: (i, 0)
    fixed = lambda i: (0, 0)
    resident = lambda w, l: _layer_spec(l, w.shape[1:], fixed, pipeline_mode=pl.Buffered(1))
    operands, specs = [], []
    if mixer_out is not None:
        o, w_mix, mix_layer = mixer_out
        operands += [o, w_mix]
        specs += [pl.BlockSpec((o.shape[0], tm, LANES), lambda i: (0, i, 0)), resident(w_mix, mix_layer)]
    operands += [h, p, ffn_gain.reshape(1, d), w_in, w_out, gate_gain.reshape(1, d), w_gate, w_proj,
                 ple_gain.reshape(1, d)]
    specs += [pl.BlockSpec((tm, d), row), _layer_spec(layer, (tm, p.shape[2]), row),
              pl.BlockSpec((1, d), fixed), resident(w_in, layer), resident(w_out, layer),
              pl.BlockSpec((1, d), fixed), resident(w_gate, layer), resident(w_proj, layer),
              pl.BlockSpec((1, d), fixed)]
    return pl.pallas_call(
        functools.partial(_ffn_ple_kernel, has_mixer_out=mixer_out is not None),
        grid=(n // tm,),
        in_specs=specs,
        out_specs=pl.BlockSpec((tm, d), row),
        out_shape=jax.ShapeDtypeStruct((n, d), F32),
        compiler_params=_params(("parallel",)),
        name="ffn_ple",
    )(*operands)


def _diff_lambda_init(layer_idx):
    return 0.8 - 0.6 * math.exp(-0.3 * layer_idx)


def _slab_gains(groups):
    rows = [jnp.tile(g, LANES // g.shape[0]) for g, count in groups for _ in range(count)]
    return jnp.stack(rows)[:, None, :].astype(F32)


def kernel(x, p, mix_norm, ffn_norm, a_w_in, a_w_out, b_w_in, b_q_norm, b_k_norm, b_w_out,
           c_w_in, c_q_norm, c_k_norm, c_lambda_q1, c_lambda_k1, c_lambda_q2, c_lambda_k2,
           c_subln, c_w_out, ffn_w_in, ffn_w_out, ple_w_proj, ple_norm, ple_gate_norm, ple_w_gate):
    batch, seq, d = x.shape
    depth = p.shape[0]
    n = batch * seq
    h = x.reshape(n, d)
    bf = lambda w: w.astype(BF16)
    a_w_in, a_w_out, b_w_in, b_w_out, c_w_in, c_w_out = map(bf, (a_w_in, a_w_out, b_w_in, b_w_out,
                                                                 c_w_in, c_w_out))
    ffn_w_in, ffn_w_out, ple_w_gate, ple_w_proj = map(bf, (ffn_w_in, ffn_w_out, ple_w_gate, ple_w_proj))
    p3 = p.reshape(depth, n, PLE_DIM)
    ones = jnp.ones((LANES,), F32)
    for i in range(depth):
        kind, j = i % N_MIXERS, i // N_MIXERS
        mixer_out = None
        if kind == 0:
            proj = _norm_proj(h, mix_norm[i], a_w_in, j, PROJ_TN_A)
            h = _retention(proj, h, a_w_out, j, batch, seq)
        elif kind == 1:
            q_gain = b_q_norm[j] * (DIL_HEAD_DIM ** -0.5 * LOG2E)
            gains = _slab_gains([(q_gain, DIL_QK // LANES), (b_k_norm[j], DIL_QK // LANES),
                                 (ones, DIL_HEADS)])
            proj = _norm_proj(h, mix_norm[i], b_w_in, j, PROJ_TN_B, gains, "full", 2 * DIL_QK // LANES,
                              out_dtype=jnp.uint32)
            mixer_out = (_dilated_attention(proj, batch, seq), b_w_out, j)
        else:
            gains = _slab_gains([(c_q_norm[j] * (DIFF_HEAD_DIM ** -0.5 * LOG2E), DIFF_QK // LANES),
                                 (c_k_norm[j], DIFF_QK // LANES), (ones, DIFF_V // LANES)])
            proj = _norm_proj(h, mix_norm[i], c_w_in, j, PROJ_TN_C, gains, "half", 2 * DIFF_QK // LANES)
            o = _diff_attention(proj, c_lambda_q1[j], c_lambda_k1[j], c_lambda_q2[j],
                                c_lambda_k2[j], c_subln[j], _diff_lambda_init(i), batch, seq)
            mixer_out = (o, c_w_out, j)
        h = _ffn_ple(h, mixer_out, p3, i, ffn_norm[i], ffn_w_in, ffn_w_out,
                     ple_gate_norm[i], ple_w_gate, ple_w_proj, ple_norm[i])
    return h.reshape(batch, seq, d)
```

```python
import functools
import math

import numpy as np
import jax
import jax.numpy as jnp
from jax import lax
from jax.experimental import pallas as pl
from jax.experimental.pallas import tpu as pltpu

D_MODEL = 1024
PLE_DIM = 256
N_MIXERS = 3
RMS_EPS = 1e-6
NEG = -1e30

RET_HEADS = 4
RET_DK = 256
RET_DV = 512
RET_CHUNK = 128
RET_IN = 2 * RET_HEADS * RET_DK + 2 * RET_HEADS * RET_DV

DIL_GROUPS = ((128, 1), (512, 4), (2048, 16))
DIL_HEADS = 8
DIL_HEAD_DIM = 128
DIL_BLOCK = 128
DIL_QK = len(DIL_GROUPS) * DIL_HEADS * DIL_HEAD_DIM
DIL_IN = 2 * DIL_QK + DIL_HEADS * DIL_HEAD_DIM

DIFF_HEADS = 8
DIFF_HEAD_DIM = 64
DIFF_QK = 2 * DIFF_HEADS * DIFF_HEAD_DIM
DIFF_V = DIFF_HEADS * 2 * DIFF_HEAD_DIM
DIFF_IN = 2 * DIFF_QK + DIFF_V

FFN_HIDDEN = 2816

LANES = 128
V7X_VMEM_BYTES = 64 * 1024 * 1024
VMEM_LIMIT_BYTES = V7X_VMEM_BYTES - 8 * 1024 * 1024

PROJ_TM = 1024
PROJ_TN_A = 3072
PROJ_TN_B = 3584
PROJ_TN_C = 3072
PROJ_CHUNK = 256
PROJ_AHEAD = 2
FFN_TM = 512
FFN_SPLIT = 2
FFN_CHUNK = 1024
RET_BLOCK = 256
RET_BATCH = 2
DIL_WINDOW = 2048
DIL_HOP = 4
DIFF_T = 2048
DIFF_STRIP = 256
DIFF_AHEAD = 8

LOG2E = math.log2(math.e)

F32 = jnp.float32
BF16 = jnp.bfloat16

_NT = (((1,), (1,)), ((), ()))
_TN = (((0,), (0,)), ((), ()))


def _params(semantics):
    return pltpu.CompilerParams(dimension_semantics=semantics, vmem_limit_bytes=VMEM_LIMIT_BYTES)


def _layer_spec(layer, block, index_map, **kwargs):
    return pl.BlockSpec((None,) + tuple(block), lambda *g: (layer,) + tuple(index_map(*g)), **kwargs)


def _rms(x):
    return x * lax.rsqrt(jnp.mean(x * x, axis=-1, keepdims=True) + RMS_EPS)


def _sigmoid(x):
    return 1.0 / (1.0 + jnp.exp(-x))


def _norm_proj_kernel(x_ref, g_ref, w_ref, hg_ref, o_ref, xn_ref, *, head_norm, n_norm_slabs):
    j = pl.program_id(1)

    @pl.when(j == 0)
    def _():
        xn_ref[...] = (_rms(x_ref[...]) * g_ref[...]).astype(BF16)

    def full_norm(z, gain):
        return _rms(z) * gain

    def half_norm(z, gain):
        lo = lax.broadcasted_iota(jnp.int32, z.shape, 1) < (LANES // 2)
        zz = z * z
        ss_lo = jnp.sum(jnp.where(lo, zz, 0.0), axis=-1, keepdims=True)
        ss_hi = jnp.sum(jnp.where(lo, 0.0, zz), axis=-1, keepdims=True)
        inv = jnp.where(lo, lax.rsqrt(ss_lo / (LANES // 2) + RMS_EPS),
                        lax.rsqrt(ss_hi / (LANES // 2) + RMS_EPS))
        return z * inv * gain

    norm = {None: None, "full": full_norm, "half": half_norm}[head_norm]
    tile_slabs = o_ref.shape[0]

    def emit(c, y):
        for s in range(PROJ_CHUNK // LANES):
            slab = c * (PROJ_CHUNK // LANES) + s
            z = y[:, s * LANES:(s + 1) * LANES]
            if norm is not None:
                z = jnp.where(j * tile_slabs + slab < n_norm_slabs, norm(z, hg_ref[slab]), z)
            z = z.astype(BF16)
            o_ref[slab] = z if o_ref.dtype == BF16 else pltpu.bitcast(z, o_ref.dtype)

    xn = xn_ref[...]
    n_chunks = w_ref.shape[1] // PROJ_CHUNK
    chunk = lambda c: jnp.dot(xn, w_ref[:, c * PROJ_CHUNK:(c + 1) * PROJ_CHUNK], preferred_element_type=F32)
    pending = {c: chunk(c) for c in range(min(PROJ_AHEAD, n_chunks))}
    for c in range(n_chunks):
        y = pending.pop(c)
        if c + PROJ_AHEAD < n_chunks:
            pending[c + PROJ_AHEAD] = chunk(c + PROJ_AHEAD)
        emit(c, y)


def _norm_proj(h, gain, w, layer, tn, slab_gains=None, head_norm=None, n_norm_slabs=0, out_dtype=BF16):
    n, d = h.shape
    n_out = w.shape[2]
    slabs = tn // LANES
    if slab_gains is None:
        slab_gains = jnp.ones((n_out // LANES, 1, LANES), F32)
    pack = 1 if out_dtype == BF16 else 2
    kern = functools.partial(_norm_proj_kernel, head_norm=head_norm, n_norm_slabs=n_norm_slabs)
    return pl.pallas_call(
        kern,
        grid=(n // PROJ_TM, n_out // tn),
        in_specs=[
            pl.BlockSpec((PROJ_TM, d), lambda i, j: (i, 0)),
            pl.BlockSpec((1, d), lambda i, j: (0, 0)),
            _layer_spec(layer, (d, tn), lambda i, j: (0, j)),
            pl.BlockSpec((slabs, 1, LANES), lambda i, j: (j, 0, 0)),
        ],
        out_specs=pl.BlockSpec((slabs, PROJ_TM // pack, LANES), lambda i, j: (j, i, 0)),
        out_shape=jax.ShapeDtypeStruct((n_out // LANES, n // pack, LANES), out_dtype),
        scratch_shapes=[pltpu.VMEM((PROJ_TM, d), BF16)],
        compiler_params=_params(("parallel", "arbitrary")),
        name="norm_proj",
    )(h, gain.reshape(1, d), w, slab_gains)


def _retention_tables():
    h = np.arange(RET_HEADS, dtype=np.float32)
    log_g = np.log(np.float32(1.0) - np.float32(2.0) ** (np.float32(-5.0) - h)).astype(np.float32)
    pos = np.arange(RET_BLOCK, dtype=np.float32)
    rel = pos[:, None] - pos[None, :]
    scale = np.float32(RET_DK ** -0.5)
    din = np.where(rel >= 0, np.exp(np.maximum(rel, 0.0)[None] * log_g[:, None, None]), 0.0)
    dq = np.exp((pos + 1.0)[None] * log_g[:, None])
    dk = np.exp((RET_BLOCK - 1.0 - pos)[None] * log_g[:, None])
    dchunk = np.exp(RET_BLOCK * log_g)
    return ((din * scale).astype(np.float32), dq.astype(np.float32)[:, :, None],
            (dk * scale).astype(np.float32)[:, :, None], tuple(float(v) for v in dchunk))


def _retention_kernel(q_ref, k_ref, v_ref, g_ref, h_ref, wo_ref, din_ref, dq_ref, dk_ref,
                      o_ref, r_ref, *, dchunk):
    @pl.when(pl.program_id(1) == 0)
    def _():
        r_ref[...] = jnp.zeros_like(r_ref)

    qs, vs = RET_DK // LANES, RET_DV // LANES
    chains = [(bb, hh) for bb in range(RET_BATCH) for hh in range(RET_HEADS)]

    def cat(ref, bb, first, count):
        return jnp.concatenate([ref[first + s, bb] for s in range(count)], axis=-1)

    qc = [cat(q_ref, bb, hh * qs, qs) for bb, hh in chains]
    kc = [cat(k_ref, bb, hh * qs, qs) for bb, hh in chains]
    vc = [cat(v_ref, bb, hh * vs, vs) for bb, hh in chains]
    att = [(lax.dot_general(qc[i], kc[i], _NT, preferred_element_type=F32) * din_ref[hh]).astype(BF16)
           for i, (bb, hh) in enumerate(chains)]
    state = [r_ref[bb, hh] for bb, hh in chains]
    y = [jnp.dot(att[i], vc[i], preferred_element_type=F32)
         + jnp.dot(qc[i], state[i].astype(BF16), preferred_element_type=F32) * dq_ref[hh]
         for i, (bb, hh) in enumerate(chains)]
    for i, (bb, hh) in enumerate(chains):
        kd = (kc[i].astype(F32) * dk_ref[hh]).astype(BF16)
        r_ref[bb, hh] = state[i] * dchunk[hh] + lax.dot_general(kd, vc[i], _TN,
                                                                preferred_element_type=F32)
    out = [h_ref[bb] for bb in range(RET_BATCH)]
    for i, (bb, hh) in enumerate(chains):
        gc = cat(g_ref, bb, hh * vs, vs).astype(F32)
        gated = (gc * _sigmoid(gc) * _rms(y[i])).astype(BF16)
        out[bb] = out[bb] + jnp.dot(gated, wo_ref[hh * RET_DV:(hh + 1) * RET_DV, :],
                                    preferred_element_type=F32)
    for bb in range(RET_BATCH):
        o_ref[bb] = out[bb]


def _retention(proj, h, w_out, layer, batch, seq):
    n, d = h.shape
    din, dq, dk, dchunk = _retention_tables()
    p4 = proj.reshape(proj.shape[0], batch, seq, LANES)
    h3 = h.reshape(batch, seq, d)
    nq = RET_HEADS * RET_DK // LANES
    nv = RET_HEADS * RET_DV // LANES
    t, nb = RET_BLOCK, RET_BATCH
    slab_spec = lambda cnt, blk: pl.BlockSpec((cnt, nb, t, LANES), lambda b, i: (blk, b, i, 0))
    const3 = lambda a: pl.BlockSpec(a.shape, lambda b, i: (0, 0, 0))
    out = pl.pallas_call(
        functools.partial(_retention_kernel, dchunk=dchunk),
        grid=(batch // nb, seq // t),
        in_specs=[
            slab_spec(nq, 0), slab_spec(nq, 1), slab_spec(nv, 1), slab_spec(nv, 2),
            pl.BlockSpec((nb, t, d), lambda b, i: (b, i, 0)),
            _layer_spec(layer, w_out.shape[1:], lambda b, i: (0, 0)),
            const3(din), const3(dq), const3(dk),
        ],
        out_specs=pl.BlockSpec((nb, t, d), lambda b, i: (b, i, 0)),
        out_shape=jax.ShapeDtypeStruct((batch, seq, d), F32),
        scratch_shapes=[pltpu.VMEM((nb, RET_HEADS, RET_DK, RET_DV), F32)],
        compiler_params=_params(("parallel", "arbitrary")),
        name="retention",
    )(p4, p4, p4, p4, h3, w_out, jnp.asarray(din), jnp.asarray(dq), jnp.asarray(dk))
    return out.reshape(n, d)


def _alibi_slopes(n):
    ratio = 2.0 ** (-8.0 / n)
    return np.array([ratio ** (i + 1) for i in range(n)], dtype=np.float32)


def _dilated_tables():
    slopes = _alibi_slopes(DIL_HEADS)
    rel = (DIL_BLOCK + np.arange(DIL_BLOCK))[:, None] - np.arange(2 * DIL_BLOCK)[None, :]
    tabs = []
    for window, dilation in DIL_GROUPS:
        valid = (rel >= 0) & (rel <= window // dilation)
        bias = -(slopes[:, None, None] * (rel * dilation).astype(np.float32)[None]) * np.float32(LOG2E)
        tabs.append(np.where(valid[None], bias, np.float32(NEG)).astype(np.float32))
    return np.stack(tabs, axis=1)


def _dilated_kernel(q0, q1, q2, k0, k1, k2, kh0, kh1, kh2, v_ref, vh_ref, tab_ref, o_ref,
                    og_ref, lg_ref, stage_ref):
    first_window = pl.program_id(2) == 0
    in_prev_block = lax.broadcasted_iota(jnp.int32, (DIL_BLOCK, 2 * DIL_BLOCK), 1) < DIL_BLOCK
    ones = jnp.ones((2 * DIL_BLOCK, LANES), BF16)

    def attend(qb, keys, vals, tab):
        s = lax.dot_general(qb, keys, _NT, preferred_element_type=F32) + tab
        m = jnp.max(s, axis=-1, keepdims=True)
        p = jnp.exp2(s - m).astype(BF16)
        res = jnp.dot(p, jnp.concatenate([vals, ones], axis=1), preferred_element_type=F32)
        den = res[:, LANES:]
        return res[:, :LANES] / den, m + jnp.log2(den)

    for g, (q_ref, k_ref, kh_ref) in enumerate(((q0, k0, kh0), (q1, k1, kh1), (q2, k2, kh2))):
        dilation = DIL_GROUPS[g][1]
        span = DIL_BLOCK * dilation
        n_blocks = DIL_WINDOW // span
        tab = tab_ref[g]
        tab_first = jnp.where(jnp.logical_and(first_window, in_prev_block), NEG, tab)
        if dilation == 1:
            streams = [lambda ref, start: pltpu.bitcast(ref[pl.ds(start // 2, DIL_BLOCK // 2), :], BF16)]
        else:
            def pair(ref, start, half):
                words = ref[pl.ds(start // 2, DIL_BLOCK, stride=dilation // 2), :]
                bits = (words << 16) if half == 0 else (words & jnp.uint32(0xFFFF0000))
                return pltpu.bitcast(bits, F32).astype(BF16)
            streams = [functools.partial(pair, half=0), functools.partial(pair, half=1)]
        two_hops = dilation > DIL_HOP
        part = DIL_WINDOW // DIL_HOP

        def out_rows(r, blk):
            if dilation == 1:
                return pl.ds(blk * span, DIL_BLOCK)
            if two_hops:
                return pl.ds((r % DIL_HOP) * part + r // DIL_HOP, DIL_BLOCK, stride=DIL_HOP)
            return pl.ds(r + blk * span, DIL_BLOCK, stride=dilation)

        for r0 in range(0, dilation, len(streams)):
            for half, load in enumerate(streams):
                r = r0 + half
                kb = [load(kh_ref, r0)] + [load(k_ref, r0 + blk * span) for blk in range(n_blocks)]
                vb = [load(vh_ref, DIL_WINDOW - span + r0)] + [load(v_ref, r0 + blk * span)
                                                                for blk in range(n_blocks)]
                for blk in range(n_blocks):
                    o, lse = attend(load(q_ref, r0 + blk * span),
                                    jnp.concatenate([kb[blk], kb[blk + 1]], axis=0),
                                    jnp.concatenate([vb[blk], vb[blk + 1]], axis=0),
                                    tab_first if blk == 0 else tab)
                    if two_hops:
                        stage_ref[0, out_rows(r, blk), :] = o
                        stage_ref[1, out_rows(r, blk), :] = lse
                    else:
                        og_ref[g, out_rows(r, blk), :] = o
                        lg_ref[g, out_rows(r, blk), :] = lse
        if two_hops:
            assert n_blocks == 1 and dilation == DIL_HOP * DIL_HOP
            for c in range(DIL_HOP):
                og_ref[g, pl.ds(c, part, stride=DIL_HOP), :] = stage_ref[0, c * part:(c + 1) * part, :]
                lg_ref[g, pl.ds(c, part, stride=DIL_HOP), :] = stage_ref[1, c * part:(c + 1) * part, :]
    l0, l1, l2 = lg_ref[0], lg_ref[1], lg_ref[2]
    m = jnp.maximum(jnp.maximum(l0, l1), l2)
    e0, e1, e2 = jnp.exp2(l0 - m), jnp.exp2(l1 - m), jnp.exp2(l2 - m)
    mix = e0 * og_ref[0] + e1 * og_ref[1] + e2 * og_ref[2]
    o_ref[...] = (mix / (e0 + e1 + e2)).astype(o_ref.dtype)


def _dilated_attention(proj, batch, seq):
    n = batch * seq
    p4 = proj.reshape(proj.shape[0], batch, seq // 2, LANES)
    n_groups = len(DIL_GROUPS)
    v0 = 2 * n_groups * DIL_HEADS

    def cur(first):
        return pl.BlockSpec((None, None, DIL_WINDOW // 2, LANES), lambda b, hh, w: (first + hh, b, w, 0))

    def halo(first, rows):
        per_window = DIL_WINDOW // rows
        return pl.BlockSpec((None, None, rows // 2, LANES),
                            lambda b, hh, w: (first + hh, b, jnp.maximum(w * per_window - 1, 0), 0))

    q_specs = [cur(g * DIL_HEADS) for g in range(n_groups)]
    k_specs = [cur((n_groups + g) * DIL_HEADS) for g in range(n_groups)]
    kh_specs = [halo((n_groups + g) * DIL_HEADS, DIL_BLOCK * DIL_GROUPS[g][1]) for g in range(n_groups)]
    tabs = _dilated_tables()
    tab_spec = pl.BlockSpec((None,) + tabs.shape[1:], lambda b, hh, w: (hh, 0, 0, 0))
    o = pl.pallas_call(
        _dilated_kernel,
        grid=(batch, DIL_HEADS, seq // DIL_WINDOW),
        in_specs=q_specs + k_specs + kh_specs + [cur(v0), halo(v0, DIL_WINDOW), tab_spec],
        out_specs=pl.BlockSpec((None, None, DIL_WINDOW, LANES), lambda b, hh, w: (hh, b, w, 0)),
        out_shape=jax.ShapeDtypeStruct((DIL_HEADS, batch, seq, LANES), BF16),
        scratch_shapes=[pltpu.VMEM((n_groups, DIL_WINDOW, LANES), F32),
                        pltpu.VMEM((n_groups, DIL_WINDOW, LANES), F32),
                        pltpu.VMEM((2, DIL_WINDOW, LANES), F32)],
        compiler_params=_params(("parallel", "parallel", "arbitrary")),
        name="dilated_attention",
    )(*([p4] * 11), jnp.asarray(tabs))
    return o.reshape(DIL_HEADS, n, LANES)


def _diff_steps(seq):
    qi, ki = [], []
    for a in range(seq // DIFF_T):
        for b in range(a + 1):
            qi.append(a)
            ki.append(b)
    return np.asarray(qi, np.int32), np.asarray(ki, np.int32)


def _diff_alibi_operands():
    import ml_dtypes
    bf16 = ml_dtypes.bfloat16
    rem = (_alibi_slopes(DIFF_HEADS) * np.float32(LOG2E)).astype(np.float32)
    pieces = []
    for _ in range(3):
        piece = rem.astype(bf16).astype(np.float32)
        pieces.append(piece)
        rem = (rem - piece).astype(np.float32)
    if np.any(rem != 0):
        raise ValueError("slope * log2(e) does not split into three bf16 pieces")
    idx = np.arange(DIFF_T)
    parts = [(idx % 256).astype(np.float32), (idx - idx % 256).astype(np.float32)]
    qa = np.zeros((DIFF_HEADS, DIFF_T, LANES), np.float32)
    kb = np.zeros((DIFF_HEADS, DIFF_T, LANES), np.float32)
    col = 0
    for piece in pieces:
        for part in parts:
            qa[:, :, col] = part[None, :]
            kb[:, :, col] = -piece[:, None]
            qa[:, :, col + 1] = piece[:, None]
            kb[:, :, col + 1] = part[None, :]
            col += 2
    return qa.astype(bf16), kb.astype(bf16)


def _diff_kernel(qi_tab, ki_tab, q_ref, k_ref, v_ref, qa_ref, kb_ref, slope_ref, lq1, lk1, lq2, lk2,
                 sg_ref, o_ref, m1, l1, a1, m2, l2, a2, *, lambda_init):
    step = pl.program_id(2)
    qi = qi_tab[step]
    ki = ki_tab[step]
    slope = slope_ref[...]

    @pl.when(ki == 0)
    def _():
        for m, l, a in ((m1, l1, a1), (m2, l2, a2)):
            m[...] = jnp.full_like(m, NEG)
            l[...] = jnp.zeros_like(l)
            a[...] = jnp.zeros_like(a)

    offset = slope * ((qi - ki) * DIFF_T).astype(F32)

    q = q_ref[...]
    lo = lax.broadcasted_iota(jnp.int32, q.shape, 1) < DIFF_HEAD_DIM
    zero = jnp.zeros_like(q)
    qa = qa_ref[...]
    keys = jnp.concatenate([k_ref[...], kb_ref[...]], axis=1)
    v = v_ref[...]

    n_strips = DIFF_T // DIFF_STRIP
    units = [(s, c) for c in range(n_strips) for s in range(2)]
    stats = ((m1, l1, a1), (m2, l2, a2))

    def update(on_diagonal):
        queries = [jnp.concatenate([jnp.where(lo, q, zero), qa], axis=1),
                   jnp.concatenate([jnp.where(lo, zero, q), qa], axis=1)]

        def n_keys(c):
            return (c + 1) * DIFF_STRIP if on_diagonal else DIFF_T

        def scores(u):
            s, c = units[u]
            return lax.dot_general(keys[:n_keys(c)], queries[s][c * DIFF_STRIP:(c + 1) * DIFF_STRIP],
                                   _NT, preferred_element_type=F32)

        pending = {u: scores(u) for u in range(DIFF_AHEAD)}
        for u, (s, c) in enumerate(units):
            m_ref, l_ref, a_ref = stats[s]
            cols = slice(c * DIFF_STRIP, (c + 1) * DIFF_STRIP)
            t = pending.pop(u)
            if on_diagonal:
                key = lax.broadcasted_iota(jnp.int32, t.shape, 0)
                qry = lax.broadcasted_iota(jnp.int32, t.shape, 1) + c * DIFF_STRIP
                t = jnp.where(key <= qry, t, NEG)
            m_old = m_ref[:, cols]
            m_new = jnp.maximum(m_old, jnp.max(t, axis=0, keepdims=True) - offset[:, cols])
            alpha = jnp.exp2(m_old - m_new)
            p = jnp.exp2(t - (m_new + offset[:, cols]))
            l_ref[:, cols] = alpha * l_ref[:, cols] + jnp.sum(p, axis=0, keepdims=True)
            a_ref[:, cols] = a_ref[:, cols] * alpha + lax.dot_general(
                v[:n_keys(c)], p.astype(BF16), _TN, preferred_element_type=F32)
            m_ref[:, cols] = m_new
            if u + DIFF_AHEAD < len(units):
                pending[u + DIFF_AHEAD] = scores(u + DIFF_AHEAD)

    pl.when(ki < qi)(lambda: update(False))
    pl.when(ki == qi)(lambda: update(True))

    @pl.when(ki == qi)
    def _():
        lam = (jnp.exp(jnp.sum(lq1[...] * lk1[...], axis=-1, keepdims=True))
               - jnp.exp(jnp.sum(lq2[...] * lk2[...], axis=-1, keepdims=True)) + lambda_init)
        o_t = a1[...] / l1[...] - lam * (a2[...] / l2[...])
        o = _rms(o_t.T) * sg_ref[...] * (1.0 - lambda_init)
        o_ref[...] = o.astype(o_ref.dtype)


def _diff_attention(proj, lq1, lk1, lq2, lk2, subln, lambda_init, batch, seq):
    n = batch * seq
    t = DIFF_T
    p4 = proj.reshape(proj.shape[0], batch, seq, LANES)
    qi_tab, ki_tab = _diff_steps(seq)
    qa, kb = _diff_alibi_operands()
    slopes = np.repeat((_alibi_slopes(DIFF_HEADS) * np.float32(LOG2E))[:, None, None], t, axis=2)
    vec = lambda a: a.reshape(1, -1).astype(F32)
    small = lambda a: pl.BlockSpec(a.shape, lambda b, hh, s, qt, kt: (0, 0))
    lqs = [vec(lq1), vec(lk1), vec(lq2), vec(lk2)]
    sg = vec(subln)
    grid_spec = pltpu.PrefetchScalarGridSpec(
        num_scalar_prefetch=2,
        grid=(batch, DIFF_HEADS, len(qi_tab)),
        in_specs=[
            pl.BlockSpec((None, None, t, LANES), lambda b, hh, s, qt, kt: (hh, b, qt[s], 0)),
            pl.BlockSpec((None, None, t, LANES), lambda b, hh, s, qt, kt: (DIFF_HEADS + hh, b, kt[s], 0)),
            pl.BlockSpec((None, None, t, LANES),
                         lambda b, hh, s, qt, kt: (2 * DIFF_HEADS + hh, b, kt[s], 0)),
            pl.BlockSpec((None, t, LANES), lambda b, hh, s, qt, kt: (hh, 0, 0)),
            pl.BlockSpec((None, t, LANES), lambda b, hh, s, qt, kt: (hh, 0, 0)),
            pl.BlockSpec((None, 1, t), lambda b, hh, s, qt, kt: (hh, 0, 0)),
            small(lqs[0]), small(lqs[1]), small(lqs[2]), small(lqs[3]), small(sg),
        ],
        out_specs=pl.BlockSpec((None, None, t, LANES), lambda b, hh, s, qt, kt: (hh, b, qt[s], 0)),
        scratch_shapes=[
            pltpu.VMEM((1, t), F32), pltpu.VMEM((1, t), F32), pltpu.VMEM((2 * DIFF_HEAD_DIM, t), F32),
            pltpu.VMEM((1, t), F32), pltpu.VMEM((1, t), F32), pltpu.VMEM((2 * DIFF_HEAD_DIM, t), F32),
        ],
    )
    o = pl.pallas_call(
        functools.partial(_diff_kernel, lambda_init=lambda_init),
        grid_spec=grid_spec,
        out_shape=jax.ShapeDtypeStruct((DIFF_HEADS, batch, seq, LANES), BF16),
        compiler_params=_params(("parallel", "parallel", "arbitrary")),
        name="diff_attention",
    )(jnp.asarray(qi_tab), jnp.asarray(ki_tab), p4, p4, p4, jnp.asarray(qa), jnp.asarray(kb),
      jnp.asarray(slopes), *lqs, sg)
    return o.reshape(DIFF_HEADS, n, LANES)


def _ffn_ple_kernel(*refs, has_mixer_out):
    if has_mixer_out:
        o_ref, wm_ref, *refs = refs
    x_ref, p_ref, g_ref, wi_ref, wo_ref, gg_ref, wg_ref, wp_ref, pg_ref, out_ref = refs
    rows = x_ref.shape[0] // FFN_SPLIT
    groups = [slice(i * rows, (i + 1) * rows) for i in range(FFN_SPLIT)]
    x = [x_ref[r, :] for r in groups]
    if has_mixer_out:
        for i, r in enumerate(groups):
            o = jnp.concatenate([o_ref[s, r, :] for s in range(o_ref.shape[0])], axis=-1)
            x[i] = x[i] + jnp.dot(o, wm_ref[...], preferred_element_type=F32)
    xn = [(_rms(v) * g_ref[...]).astype(BF16) for v in x]
    e = [_rms(jnp.dot(p_ref[r, :].astype(BF16), wp_ref[...], preferred_element_type=F32)) * pg_ref[...]
         for r in groups]
    h = list(x)
    for c0 in range(0, FFN_HIDDEN, FFN_CHUNK):
        c1 = min(c0 + FFN_CHUNK, FFN_HIDDEN)
        ups = [(jnp.dot(v, wi_ref[:, c0:c1], preferred_element_type=F32),
                jnp.dot(v, wi_ref[:, FFN_HIDDEN + c0:FFN_HIDDEN + c1], preferred_element_type=F32))
               for v in xn]
        for i, (a, b) in enumerate(ups):
            act = (a * _sigmoid(a) * b).astype(BF16)
            h[i] = h[i] + jnp.dot(act, wo_ref[c0:c1, :], preferred_element_type=F32)
    for i, r in enumerate(groups):
        hn = (_rms(h[i]) * gg_ref[...]).astype(BF16)
        gate = _sigmoid(jnp.dot(hn, wg_ref[...], preferred_element_type=F32))
        out_ref[r, :] = h[i] + gate * e[i]


def _ffn_ple(h, mixer_out, p, layer, ffn_gain, w_in, w_out, gate_gain, w_gate, w_proj, ple_gain):
    n, d = h.shape
    tm = FFN_TM
    row = lambda i: (i, 0)
    fixed = lambda i: (0, 0)
    resident = lambda w, l: _layer_spec(l, w.shape[1:], fixed, pipeline_mode=pl.Buffered(1))
    operands, specs = [], []
    if mixer_out is not None:
        o, w_mix, mix_layer = mixer_out
        operands += [o, w_mix]
        specs += [pl.BlockSpec((o.shape[0], tm, LANES), lambda i: (0, i, 0)), resident(w_mix, mix_layer)]
    operands += [h, p, ffn_gain.reshape(1, d), w_in, w_out, gate_gain.reshape(1, d), w_gate, w_proj,
                 ple_gain.reshape(1, d)]
    specs += [pl.BlockSpec((tm, d), row), _layer_spec(layer, (tm, p.shape[2]), row),
              pl.BlockSpec((1, d), fixed), resident(w_in, layer), resident(w_out, layer),
              pl.BlockSpec((1, d), fixed), resident(w_gate, layer), resident(w_proj, layer),
              pl.BlockSpec((1, d), fixed)]
    return pl.pallas_call(
        functools.partial(_ffn_ple_kernel, has_mixer_out=mixer_out is not None),
        grid=(n // tm,),
        in_specs=specs,
        out_specs=pl.BlockSpec((tm, d), row),
        out_shape=jax.ShapeDtypeStruct((n, d), F32),
        compiler_params=_params(("parallel",)),
        name="ffn_ple",
    )(*operands)


def _diff_lambda_init(layer_idx):
    return 0.8 - 0.6 * math.exp(-0.3 * layer_idx)


def _slab_gains(groups):
    rows = [jnp.tile(g, LANES // g.shape[0]) for g, count in groups for _ in range(count)]
    return jnp.stack(rows)[:, None, :].astype(F32)


def kernel(x, p, mix_norm, ffn_norm, a_w_in, a_w_out, b_w_in, b_q_norm, b_k_norm, b_w_out,
           c_w_in, c_q_norm, c_k_norm, c_lambda_q1, c_lambda_k1, c_lambda_q2, c_lambda_k2,
           c_subln, c_w_out, ffn_w_in, ffn_w_out, ple_w_proj, ple_norm, ple_gate_norm, ple_w_gate):
    batch, seq, d = x.shape
    depth = p.shape[0]
    n = batch * seq
    h = x.reshape(n, d)
    bf = lambda w: w.astype(BF16)
    a_w_in, a_w_out, b_w_in, b_w_out, c_w_in, c_w_out = map(bf, (a_w_in, a_w_out, b_w_in, b_w_out,
                                                                 c_w_in, c_w_out))
    ffn_w_in, ffn_w_out, ple_w_gate, ple_w_proj = map(bf, (ffn_w_in, ffn_w_out, ple_w_gate, ple_w_proj))
    p3 = p.reshape(depth, n, PLE_DIM)
    ones = jnp.ones((LANES,), F32)
    for i in range(depth):
        kind, j = i % N_MIXERS, i // N_MIXERS
        mixer_out = None
        if kind == 0:
            proj = _norm_proj(h, mix_norm[i], a_w_in, j, PROJ_TN_A)
            h = _retention(proj, h, a_w_out, j, batch, seq)
        elif kind == 1:
            q_gain = b_q_norm[j] * (DIL_HEAD_DIM ** -0.5 * LOG2E)
            gains = _slab_gains([(q_gain, DIL_QK // LANES), (b_k_norm[j], DIL_QK // LANES),
                                 (ones, DIL_HEADS)])
            proj = _norm_proj(h, mix_norm[i], b_w_in, j, PROJ_TN_B, gains, "full", 2 * DIL_QK // LANES,
                              out_dtype=jnp.uint32)
            mixer_out = (_dilated_attention(proj, batch, seq), b_w_out, j)
        else:
            gains = _slab_gains([(c_q_norm[j] * (DIFF_HEAD_DIM ** -0.5 * LOG2E), DIFF_QK // LANES),
                                 (c_k_norm[j], DIFF_QK // LANES), (ones, DIFF_V // LANES)])
            proj = _norm_proj(h, mix_norm[i], c_w_in, j, PROJ_TN_C, gains, "half", 2 * DIFF_QK // LANES)
            o = _diff_attention(proj, c_lambda_q1[j], c_lambda_k1[j], c_lambda_q2[j],
                                c_lambda_k2[j], c_subln[j], _diff_lambda_init(i), batch, seq)
            mixer_out = (o, c_w_out, j)
        h = _ffn_ple(h, mixer_out, p3, i, ffn_norm[i], ffn_w_in, ffn_w_out,
                     ple_gate_norm[i], ple_w_gate, ple_w_proj, ple_norm[i])
    return h.reshape(batch, seq, d)
```
